```python
import math
import jax, jax.numpy as jnp
from jax import lax
import numpy as np

D_MODEL = 1024
BATCH = 4
SEQ = 4096
DEPTH = 2
DEC_BATCH = 128
DEC_SEQ = 4
PAST_LEN = 16384
PAGE_SIZE = 128

MLA_H = 8
MLA_Q_LORA = 256
MLA_KV_LORA = 128
MLA_NOPE = 64
MLA_ROPE = 32
MLA_V = 64
MLA_SCALE = (MLA_NOPE + MLA_ROPE) ** -0.5
DIFF_H = 4
DIFF_D = 64
DIFF_SCALE = DIFF_D ** -0.5
DSA_H = 8
DSA_D = 64
DSA_SCALE = DSA_D ** -0.5
IDX_H = 8
IDX_D = 64
IDX_TOPK = 256
IDX_W_SCALE = (IDX_H ** -0.5) * (IDX_D ** -0.5)
N_BRANCH = 3
BRANCH_W = 512
N_GROUPS = 4
EXPERTS_PER_GROUP = 8
N_EXPERTS = N_GROUPS * EXPERTS_PER_GROUP
TOP_E = 2
D_FF = 256
ROPE_THETA = 10000.0
Q_BLOCK = 128
EPS = 1e-5
ALPHA = (2 * DEPTH) ** 0.25
BETA = (8 * DEPTH) ** -0.25
IN_WIDTHS = (MLA_Q_LORA, MLA_KV_LORA, MLA_ROPE,
             DIFF_H * 2 * DIFF_D, 2 * DIFF_D, 2 * DIFF_D,
             DSA_H * DSA_D, DSA_D, DSA_D, IDX_H * IDX_D, IDX_D, IDX_H)
IN_WIDTH = sum(IN_WIDTHS)
MLA_CACHE_W = MLA_KV_LORA + MLA_ROPE
DIFF_CACHE_W = 4 * DIFF_D
DSA_CACHE_W = 2 * DSA_D + IDX_D

kernel_name = 'hybrid_mla_diff_dsa_hmoe_decode_step'


def layer_norm(x, g, b):
    xf = x.astype(jnp.float32)
    mu = jnp.mean(xf, axis=-1, keepdims=True)
    var = jnp.mean(jnp.square(xf - mu), axis=-1, keepdims=True)
    return ((xf - mu) * lax.rsqrt(var + EPS) * g + b).astype(x.dtype)


def rms_norm(x, g):
    xf = x.astype(jnp.float32)
    return (xf * lax.rsqrt(jnp.mean(jnp.square(xf), axis=-1, keepdims=True) + EPS) * g).astype(x.dtype)


def rope(x, pos):
    half = x.shape[-1] // 2
    inv = 1.0 / (ROPE_THETA ** (jnp.arange(half, dtype=jnp.float32) / half))
    ang = pos.astype(jnp.float32)[:, None] * inv[None, :]
    shape = (1, pos.shape[0]) + (1,) * (x.ndim - 3) + (half,)
    cos, sin = jnp.cos(ang).reshape(shape), jnp.sin(ang).reshape(shape)
    x1, x2 = x[..., :half].astype(jnp.float32), x[..., half:].astype(jnp.float32)
    return jnp.concatenate([x1 * cos - x2 * sin, x2 * cos + x1 * sin], axis=-1).astype(x.dtype)


def causal(q_pos, k_pos):
    return k_pos[None, :] <= q_pos[:, None]


def over_query_blocks(fn, qs, q_pos):
    sq = q_pos.shape[0]
    if sq <= Q_BLOCK or sq % Q_BLOCK != 0:
        return fn(qs, q_pos)
    nb = sq // Q_BLOCK
    qs_b = tuple(jnp.moveaxis(q.reshape((q.shape[0], nb, Q_BLOCK) + q.shape[2:]), 1, 0) for q in qs)
    out = lax.map(lambda a: fn(a[0], a[1]), (qs_b, q_pos.reshape(nb, Q_BLOCK)))
    out = jnp.moveaxis(out, 0, 1)
    return out.reshape((out.shape[0], sq) + out.shape[3:])


def mla_attend(q_lat, q_rope, q_pos, ckv, krope, k_pos):
    s = (jnp.einsum('bqhc,bkc->bhqk', q_lat, ckv) + jnp.einsum('bqhr,bkr->bhqk', q_rope, krope)).astype(jnp.float32) * MLA_SCALE
    p = jax.nn.softmax(jnp.where(causal(q_pos, k_pos), s, -jnp.inf), axis=-1).astype(ckv.dtype)
    return jnp.einsum('bhqk,bkc->bqhc', p, ckv)


def diff_attend(q, q_pos, k, v, k_pos, lam):
    s = jnp.einsum('bqhjd,bkjd->bhjqk', q, k).astype(jnp.float32) * DIFF_SCALE
    p = jax.nn.softmax(jnp.where(causal(q_pos, k_pos), s, -jnp.inf), axis=-1)
    a = (p[:, :, 0] - lam * p[:, :, 1]).astype(v.dtype)
    return jnp.einsum('bhqk,bkv->bqhv', a, v)


def take_rows(rows, idx):
    return jax.vmap(lambda r, i: r[i])(rows, idx)


def dsa_attend(q, iq, iw, q_pos, k, v, ik, k_pos, topk):
    idx_logit = jax.nn.relu(jnp.einsum('bqhd,bkd->bqhk', iq, ik))
    score = jnp.einsum('bqh,bqhk->bqk', iw, idx_logit).astype(jnp.float32)
    score = jnp.where(causal(q_pos, k_pos), score, -jnp.inf)
    _, sel = lax.top_k(score, topk)
    ks, vs = take_rows(k, sel), take_rows(v, sel)
    valid = k_pos[sel] <= q_pos[None, :, None]
    s = jnp.einsum('bqhd,bqnd->bqhn', q, ks).astype(jnp.float32) * DSA_SCALE
    p = jax.nn.softmax(jnp.where(valid[:, :, None, :], s, -jnp.inf), axis=-1).astype(vs.dtype)
    return jnp.einsum('bqhn,bqnd->bqhd', p, vs)


def hier_moe(x, lp):
    b, s, d = x.shape
    t = x.reshape(b * s, d)
    pg = jax.nn.softmax((t @ lp['w_rg'] + lp['b_rg']).astype(jnp.float32), axis=-1)
    g_star = jnp.argmax(pg, axis=-1)
    pg_star = jnp.max(pg, axis=-1)
    le = (t @ lp['w_re'] + lp['b_re']).astype(jnp.float32).reshape(-1, N_GROUPS, EXPERTS_PER_GROUP)
    le_sel = jnp.take_along_axis(le, g_star[:, None, None], axis=1)[:, 0]
    w_top, i_top = lax.top_k(jax.nn.softmax(le_sel, axis=-1), TOP_E)
    gate = pg_star[:, None] * w_top / jnp.sum(w_top, axis=-1, keepdims=True)
    expert_id = g_star[:, None] * EXPERTS_PER_GROUP + i_top
    combine = jnp.sum(jax.nn.one_hot(expert_id, N_EXPERTS, dtype=jnp.float32) * gate[..., None], axis=1).astype(x.dtype)
    h = jnp.einsum('td,edf->tef', t, lp['w_e_in'])
    act = jax.nn.silu(h[..., :D_FF]) * h[..., D_FF:] * combine[..., None]
    return jnp.einsum('tef,efd->td', act, lp['w_e_out']).reshape(b, s, d)


def decoder_layer(x, q_pos, past, lp, lam_init):
    b, s, _ = x.shape
    h = x @ lp['w_in']
    pts = [int(v) for v in np.cumsum(IN_WIDTHS)[:-1]]
    c_q, c_kv, kr, dq, dk, dv, sq_, sk, sv, iq, ik, iw = jnp.split(h, pts, axis=-1)
    q = (rms_norm(c_q, lp['mla_gq']) @ lp['mla_wuq']).reshape(b, s, MLA_H, MLA_NOPE + MLA_ROPE)
    q_nope, q_rope = q[..., :MLA_NOPE], rope(q[..., MLA_NOPE:], q_pos)
    wukv = lp['mla_wukv'].reshape(MLA_KV_LORA, MLA_H, MLA_NOPE + MLA_V)
    w_uk, w_uv = wukv[..., :MLA_NOPE], wukv[..., MLA_NOPE:]
    q_lat = jnp.einsum('bshn,chn->bshc', q_nope, w_uk)
    new_mla = jnp.concatenate([rms_norm(c_kv, lp['mla_gkv']), rope(kr, q_pos)], axis=-1)
    dq = rope(dq.reshape(b, s, DIFF_H, 2, DIFF_D), q_pos)
    dk = rope(dk.reshape(b, s, 2, DIFF_D), q_pos)
    new_diff = jnp.concatenate([dk.reshape(b, s, 2 * DIFF_D), dv], axis=-1)
    sq_ = rope(sq_.reshape(b, s, DSA_H, DSA_D), q_pos)
    iq = rope(iq.reshape(b, s, IDX_H, IDX_D), q_pos)
    iw = iw * IDX_W_SCALE
    new_dsa = jnp.concatenate([rope(sk, q_pos), sv, rope(ik, q_pos)], axis=-1)
    if past is None:
        k_mla, k_diff, k_dsa = new_mla, new_diff, new_dsa
    else:
        k_mla = jnp.concatenate([past[0], new_mla], axis=1)
        k_diff = jnp.concatenate([past[1], new_diff], axis=1)
        k_dsa = jnp.concatenate([past[2], new_dsa], axis=1)
    n_keys = k_mla.shape[1]
    k_pos = jnp.arange(n_keys, dtype=jnp.int32)
    ckv, krope = k_mla[..., :MLA_KV_LORA], k_mla[..., MLA_KV_LORA:]
    dk_all = k_diff[..., :2 * DIFF_D].reshape(b, n_keys, 2, DIFF_D)
    dv_all = k_diff[..., 2 * DIFF_D:]
    sk_all, sv_all, ik_all = k_dsa[..., :DSA_D], k_dsa[..., DSA_D:2 * DSA_D], k_dsa[..., 2 * DSA_D:]
    o_lat = over_query_blocks(lambda qs, qp: mla_attend(qs[0], qs[1], qp, ckv, krope, k_pos), (q_lat, q_rope), q_pos)
    o_mla = jnp.einsum('bshc,chv->bshv', o_lat, w_uv).reshape(b, s, BRANCH_W)
    lam_p = lp['diff_lam'].astype(jnp.float32)
    lam = jnp.exp(jnp.sum(lam_p[0] * lam_p[1])) - jnp.exp(jnp.sum(lam_p[2] * lam_p[3])) + lam_init
    o_diff = over_query_blocks(lambda qs, qp: diff_attend(qs[0], qp, dk_all, dv_all, k_pos, lam), (dq,), q_pos)
    o_diff = (rms_norm(o_diff, lp['diff_subln']) * (1.0 - lam_init)).reshape(b, s, BRANCH_W)
    topk = min(IDX_TOPK, n_keys // 4)
    o_dsa = over_query_blocks(lambda qs, qp: dsa_attend(qs[0], qs[1], qs[2], qp, sk_all, sv_all, ik_all, k_pos, topk), (sq_, iq, iw), q_pos)
    o_dsa = o_dsa.reshape(b, s, BRANCH_W)
    o = jnp.stack([o_mla, o_diff, o_dsa], axis=2)
    branch = jnp.einsum('bsnw,nwd->bsnd', o, lp['w_branch'])
    gates = jax.nn.sigmoid(x @ lp['w_gate'] + lp['b_gate']).reshape(b, s, N_BRANCH, D_MODEL)
    mixed = jnp.sum(gates * branch, axis=2) @ lp['w_o']
    x = layer_norm(ALPHA * x + mixed, lp['ln1_g'], lp['ln1_b'])
    x = layer_norm(ALPHA * x + hier_moe(x, lp), lp['ln2_g'], lp['ln2_b'])
    return x, new_mla, new_diff, new_dsa


def gather_pages(pool, page_table):
    g = pool[page_table]
    return g.reshape(g.shape[0], g.shape[1] * g.shape[2], g.shape[3])


def setup_inputs(seed: int = 0) -> dict:
    key = jax.random.key(seed)
    ks = jax.random.split(key, 32)
    n_pages = PAST_LEN // PAGE_SIZE
    n_pool = (DEC_BATCH * n_pages * 5) // 4

    def nrm(k, shape, scale):
        return jax.random.normal(k, shape, jnp.float32) * scale

    page_table = jax.random.permutation(ks[5], n_pool)[: DEC_BATCH * n_pages].reshape(DEC_BATCH, n_pages).astype(jnp.int32)
    return {
        'x_prompt': nrm(ks[0], (BATCH, SEQ, D_MODEL), 1.0),
        'x_sample': nrm(ks[1], (DEC_BATCH, DEC_SEQ, D_MODEL), 1.0),
        'cache_mla': nrm(ks[2], (DEPTH, n_pool, PAGE_SIZE, MLA_CACHE_W), 1.0),
        'cache_diff': nrm(ks[3], (DEPTH, n_pool, PAGE_SIZE, DIFF_CACHE_W), 1.0),
        'cache_dsa': nrm(ks[4], (DEPTH, n_pool, PAGE_SIZE, DSA_CACHE_W), 1.0),
        'page_table': page_table,
        'w_in': nrm(ks[6], (DEPTH, D_MODEL, IN_WIDTH), D_MODEL ** -0.5),
        'mla_gq': 1.0 + nrm(ks[7], (DEPTH, MLA_Q_LORA), 0.02),
        'mla_gkv': 1.0 + nrm(ks[8], (DEPTH, MLA_KV_LORA), 0.02),
        'mla_wuq': nrm(ks[9], (DEPTH, MLA_Q_LORA, MLA_H * (MLA_NOPE + MLA_ROPE)), MLA_Q_LORA ** -0.5),
        'mla_wukv': nrm(ks[10], (DEPTH, MLA_KV_LORA, MLA_H * (MLA_NOPE + MLA_V)), MLA_KV_LORA ** -0.5),
        'diff_lam': nrm(ks[11], (DEPTH, 4, DIFF_D), 0.1),
        'diff_subln': 1.0 + nrm(ks[12], (DEPTH, 2 * DIFF_D), 0.02),
        'w_gate': nrm(ks[13], (DEPTH, D_MODEL, N_BRANCH * D_MODEL), D_MODEL ** -0.5),
        'b_gate': nrm(ks[14], (DEPTH, N_BRANCH * D_MODEL), 0.02),
        'w_branch': nrm(ks[15], (DEPTH, N_BRANCH, BRANCH_W, D_MODEL), BRANCH_W ** -0.5),
        'w_o': nrm(ks[16], (DEPTH, D_MODEL, D_MODEL), BETA * D_MODEL ** -0.5),
        'ln1_g': 1.0 + nrm(ks[17], (DEPTH, D_MODEL), 0.02),
        'ln1_b': nrm(ks[18], (DEPTH, D_MODEL), 0.02),
        'w_rg': nrm(ks[19], (DEPTH, D_MODEL, N_GROUPS), D_MODEL ** -0.5),
        'b_rg': nrm(ks[20], (DEPTH, N_GROUPS), 0.01),
        'w_re': nrm(ks[21], (DEPTH, D_MODEL, N_EXPERTS), D_MODEL ** -0.5),
        'b_re': nrm(ks[22], (DEPTH, N_EXPERTS), 0.01),
        'w_e_in': nrm(ks[23], (DEPTH, N_EXPERTS, D_MODEL, 2 * D_FF), D_MODEL ** -0.5),
        'w_e_out': nrm(ks[24], (DEPTH, N_EXPERTS, D_FF, D_MODEL), BETA * D_FF ** -0.5),
        'ln2_g': 1.0 + nrm(ks[25], (DEPTH, D_MODEL), 0.02),
        'ln2_b': nrm(ks[26], (DEPTH, D_MODEL), 0.02),
    }


def reference(x_prompt, x_sample, cache_mla, cache_diff, cache_dsa, page_table,
              w_in, mla_gq, mla_gkv, mla_wuq, mla_wukv, diff_lam, diff_subln,
              w_gate, b_gate, w_branch, w_o, ln1_g, ln1_b,
              w_rg, b_rg, w_re, b_re, w_e_in, w_e_out, ln2_g, ln2_b):
    pos_p = jnp.arange(x_prompt.shape[1], dtype=jnp.int32)
    pos_s = PAST_LEN + jnp.arange(x_sample.shape[1], dtype=jnp.int32)
    xp, xs = x_prompt, x_sample
    mla_p, mla_s, diff_p, diff_s, dsa_p, dsa_s = [], [], [], [], [], []
    for l in range(DEPTH):
        lp = {'w_in': w_in[l], 'mla_gq': mla_gq[l], 'mla_gkv': mla_gkv[l], 'mla_wuq': mla_wuq[l],
              'mla_wukv': mla_wukv[l], 'diff_lam': diff_lam[l], 'diff_subln': diff_subln[l],
              'w_gate': w_gate[l], 'b_gate': b_gate[l], 'w_branch': w_branch[l], 'w_o': w_o[l],
              'ln1_g': ln1_g[l], 'ln1_b': ln1_b[l], 'w_rg': w_rg[l], 'b_rg': b_rg[l],
              'w_re': w_re[l], 'b_re': b_re[l], 'w_e_in': w_e_in[l], 'w_e_out': w_e_out[l],
              'ln2_g': ln2_g[l], 'ln2_b': ln2_b[l]}
        lam_init = 0.8 - 0.6 * math.exp(-0.3 * l)
        xp, nm, nd, ns = decoder_layer(xp, pos_p, None, lp, lam_init)
        mla_p.append(nm); diff_p.append(nd); dsa_p.append(ns)
        past = (gather_pages(cache_mla[l], page_table), gather_pages(cache_diff[l], page_table),
                gather_pages(cache_dsa[l], page_table))
        xs, nm, nd, ns = decoder_layer(xs, pos_s, past, lp, lam_init)
        mla_s.append(nm); diff_s.append(nd); dsa_s.append(ns)
    return (xp, xs, jnp.stack(mla_p), jnp.stack(mla_s), jnp.stack(diff_p), jnp.stack(diff_s),
            jnp.stack(dsa_p), jnp.stack(dsa_s))
```

```python
import functools
import math

import numpy as np
import jax
import jax.numpy as jnp
from jax import lax
from jax.experimental import pallas as pl
from jax.experimental.pallas import tpu as pltpu

F32 = jnp.float32
BF16 = jnp.bfloat16
I32 = jnp.int32

D_MODEL = 1024
PAGE = 128
MLA_H, MLA_Q_LORA, MLA_KV_LORA, MLA_NOPE, MLA_ROPE, MLA_V = 8, 256, 128, 64, 32, 64
MLA_SCALE = (MLA_NOPE + MLA_ROPE) ** -0.5
DIFF_H, DIFF_D = 4, 64
DIFF_SCALE = DIFF_D ** -0.5
DSA_H, DSA_D = 8, 64
DSA_SCALE = DSA_D ** -0.5
IDX_H, IDX_D, IDX_TOPK = 8, 64, 256
IDX_W_SCALE = (IDX_H ** -0.5) * (IDX_D ** -0.5)
N_BRANCH, BRANCH_W = 3, 512
N_GROUPS, EXPERTS_PER_GROUP, TOP_E, D_FF = 4, 8, 2, 256
N_EXPERTS = N_GROUPS * EXPERTS_PER_GROUP
ROPE_THETA = 10000.0
EPS = 1e-5
MLA_W, DIFF_W, DSA_W = 160, 256, 192

LANES = 128
NEG = -1e30
INT_MIN = -(2 ** 31)
KEY_NINF = -2139095041
IDX_BITS = 15
VMEM_LIMIT = 52 * 1024 * 1024

C_CQ, C_DQ, C_QI, C_DKV, C_GH, C_CKV, IN_COLS = 0, 256, 768, 1792, 2048, 2304, 2432


def _cparams(sem):
    return pltpu.CompilerParams(dimension_semantics=sem, vmem_limit_bytes=VMEM_LIMIT)


def _dot(a, b):
    return jnp.dot(a, b, preferred_element_type=F32)


def _dot_t(a, b):
    return lax.dot_general(a, b, (((1,), (1,)), ((), ())), preferred_element_type=F32)


def _lane_iota(shape):
    return lax.broadcasted_iota(I32, shape, len(shape) - 1)


def _row_iota(shape):
    return lax.broadcasted_iota(I32, shape, 0)


def _proj_kernel(x_ref, w_ref, wq_ref, gq_ref, gkv_ref, cos64_ref, sin64_ref, cosh_ref, sinh_ref,
                 nmla_ref, ndiff_ref, ndsa_ref, qmla_ref, kmla_ref, qdiff_ref, kvdiff_ref, qidsa_ref,
                 kvdsa_ref, iw_ref):
    tm = x_ref.shape[0]
    xb = x_ref[...].astype(BF16)
    lane = _lane_iota((tm, LANES))
    cos64, sin64 = cos64_ref[...], sin64_ref[...]
    cosh, sinh = cosh_ref[...], sinh_ref[...]
    low64 = (lane % 64) < 32
    low32 = (lane % 32) < 16

    def rope64(v):
        partner = jnp.where(low64, pltpu.roll(v, 96, 1), pltpu.roll(v, 32, 1))
        return v * cos64 + partner * sin64

    def rope32(v):
        partner = jnp.where(low32, pltpu.roll(v, 112, 1), pltpu.roll(v, 16, 1))
        return v * cosh + partner * sinh

    def seg(lo, hi):
        return _dot(xb, w_ref[:, lo:hi])

    cq = seg(C_CQ, C_CQ + 256)
    cqn = cq * lax.rsqrt(jnp.mean(cq * cq, axis=-1, keepdims=True) + EPS) * gq_ref[...]
    q = _dot(cqn.astype(BF16), wq_ref[...])
    for h in range(MLA_H):
        qmla_ref[:, h * 256:h * 256 + 128] = q[:, h * 256:h * 256 + 128].astype(BF16)
        qmla_ref[:, h * 256 + 128:(h + 1) * 256] = rope32(q[:, h * 256 + 128:(h + 1) * 256]).astype(BF16)

    dq = seg(C_DQ, C_DQ + 512)
    for c in range(4):
        qdiff_ref[:, c * 128:(c + 1) * 128] = rope64(dq[:, c * 128:(c + 1) * 128]).astype(BF16)

    for half in range(2):
        qi = seg(C_QI + half * 512, C_QI + (half + 1) * 512)
        for c in range(4):
            cc = half * 4 + c
            qidsa_ref[:, cc * 128:(cc + 1) * 128] = rope64(qi[:, c * 128:(c + 1) * 128]).astype(BF16)

    dkv = seg(C_DKV, C_DKV + 256)
    dk = rope64(dkv[:, 0:128])
    dv = dkv[:, 128:256]
    ndiff_ref[:, 0:128] = dk
    ndiff_ref[:, 128:256] = dv
    kvdiff_ref[:, 0:128] = dk.astype(BF16)
    kvdiff_ref[:, 128:256] = dv.astype(BF16)

    gh = seg(C_GH, C_GH + 256)
    g = rope64(gh[:, 0:128])
    hraw = gh[:, 128:256]
    hr = rope32(hraw)
    sv_low = pltpu.roll(hraw, 64, 1)
    ndsa_ref[:, 0:128] = jnp.where(lane < 64, g, hraw)
    ndsa_ref[:, 128:192] = g[:, 64:128]
    kvdsa_ref[:, 0:128] = g.astype(BF16)
    kvdsa_ref[:, 128:256] = jnp.where(lane < 64, sv_low, 0.0).astype(BF16)
    iw_ref[...] = hraw[:, 32:40] * IDX_W_SCALE

    ckv = seg(C_CKV, C_CKV + 128)
    ckvn = ckv * lax.rsqrt(jnp.mean(ckv * ckv, axis=-1, keepdims=True) + EPS) * gkv_ref[...]
    nmla_ref[:, 0:128] = ckvn
    nmla_ref[:, 128:160] = hr[:, 0:32]
    kmla_ref[:, 0:128] = ckvn.astype(BF16)
    kmla_ref[:, 128:256] = jnp.where(lane < 32, hr, 0.0).astype(BF16)


def _proj_call(x, w_in_p, wq, gq, gkv, cos64, sin64, cosh, sinh, tm):
    t = x.shape[0]
    row = lambda w: pl.BlockSpec((tm, w), lambda i: (i, 0))
    full = lambda a: pl.BlockSpec(a.shape, lambda i: (0,) * a.ndim)
    out_shapes = [
        jax.ShapeDtypeStruct((t, MLA_W), F32), jax.ShapeDtypeStruct((t, DIFF_W), F32),
        jax.ShapeDtypeStruct((t, DSA_W), F32),
        jax.ShapeDtypeStruct((t, 2048), BF16), jax.ShapeDtypeStruct((t, 256), BF16),
        jax.ShapeDtypeStruct((t, 512), BF16), jax.ShapeDtypeStruct((t, 256), BF16),
        jax.ShapeDtypeStruct((t, 1024), BF16), jax.ShapeDtypeStruct((t, 256), BF16),
        jax.ShapeDtypeStruct((t, IDX_H), F32),
    ]
    return pl.pallas_call(
        _proj_kernel,
        grid=(t // tm,),
        in_specs=[row(D_MODEL), full(w_in_p), full(wq), full(gq), full(gkv),
                  row(LANES), row(LANES), row(LANES), row(LANES)],
        out_specs=[row(s.shape[1]) for s in out_shapes],
        out_shape=out_shapes,
        compiler_params=_cparams(("parallel",)),
        name="proj_in",
    )(x, w_in_p, wq, gq, gkv, cos64, sin64, cosh, sinh)


def _softmax_step(s, v, m_scr, l_scr, acc_scr):
    m_old = m_scr[...]
    m_new = jnp.maximum(m_old, jnp.max(s, axis=-1, keepdims=True))
    alpha = jnp.exp(m_old - m_new)
    p = jnp.exp(s - m_new)
    l_scr[...] = alpha * l_scr[...] + jnp.sum(p, axis=-1, keepdims=True)
    acc_scr[...] = alpha * acc_scr[...] + _dot(p.astype(BF16), v)
    m_scr[...] = m_new


def _softmax_step_t(s, vt, m_scr, l_scr, acc_scr):
    m_old = m_scr[...]
    m_new = jnp.maximum(m_old, jnp.max(s, axis=-1, keepdims=True))
    alpha = jnp.exp(m_old - m_new)
    p = jnp.exp(s - m_new)
    l_scr[...] = alpha * l_scr[...] + jnp.sum(p, axis=-1, keepdims=True)
    acc_scr[...] = alpha * acc_scr[...] + _dot_t(p.astype(BF16), vt)
    m_scr[...] = m_new


def _softmax_cols(cols, vrows, m_scr, l_scr, acc_scr):
    m_old = m_scr[...]
    m_new = m_old
    for c in cols:
        m_new = jnp.maximum(m_new, c)
    alpha = jnp.exp(m_old - m_new)
    l_new = alpha * l_scr[...]
    acc = alpha * acc_scr[...]
    for c, v in zip(cols, vrows):
        p = jnp.exp(c - m_new)
        l_new = l_new + p
        acc = acc + p * v
    l_scr[...] = l_new
    acc_scr[...] = acc
    m_scr[...] = m_new


def _init_state(m_scr, l_scr, acc_scr):
    m_scr[...] = jnp.full(m_scr.shape, NEG, F32)
    l_scr[...] = jnp.zeros(l_scr.shape, F32)
    acc_scr[...] = jnp.zeros(acc_scr.shape, F32)


def _causal_loop(qb, tq, q, key_of, val_of, m_scr, l_scr, acc_scr):
    m = q.shape[0]

    def full_block(j, carry):
        _softmax_step(_dot_t(q, key_of(j)), val_of(j), m_scr, l_scr, acc_scr)
        return carry

    lax.fori_loop(0, qb, full_block, 0)
    s = _dot_t(q, key_of(qb))
    keep = _lane_iota((m, tq)) <= (_row_iota((m, tq)) % tq)
    _softmax_step(jnp.where(keep, s, NEG), val_of(qb), m_scr, l_scr, acc_scr)


def _mla_prompt_kernel(q_ref, k_ref, wuv_ref, o_ref, m_scr, l_scr, acc_scr, *, tq):
    qb = pl.program_id(1)
    q = jnp.concatenate([q_ref[:, h * 256:(h + 1) * 256] for h in range(MLA_H)], axis=0)
    _init_state(m_scr, l_scr, acc_scr)

    def key_of(j):
        return k_ref[pl.ds(pl.multiple_of(j * tq, tq), tq), :]

    def val_of(j):
        return k_ref[pl.ds(pl.multiple_of(j * tq, tq), tq), 0:128]

    _causal_loop(qb, tq, q, key_of, val_of, m_scr, l_scr, acc_scr)
    o = (acc_scr[...] / l_scr[...]).astype(BF16)
    for h in range(MLA_H):
        o_ref[:, h * MLA_V:(h + 1) * MLA_V] = _dot(o[h * tq:(h + 1) * tq], wuv_ref[h]).astype(BF16)


def _mla_prompt_call(q, k, wuv, nb, seq, tq):
    nq = seq // tq
    m = MLA_H * tq
    return pl.pallas_call(
        functools.partial(_mla_prompt_kernel, tq=tq),
        grid=(nb, nq),
        in_specs=[pl.BlockSpec((tq, 2048), lambda b, i: (b * nq + i, 0)),
                  pl.BlockSpec((seq, 256), lambda b, i: (b, 0)),
                  pl.BlockSpec(wuv.shape, lambda b, i: (0, 0, 0))],
        out_specs=pl.BlockSpec((tq, BRANCH_W), lambda b, i: (b * nq + i, 0)),
        out_shape=jax.ShapeDtypeStruct((nb * seq, BRANCH_W), BF16),
        scratch_shapes=[pltpu.VMEM((m, 1), F32), pltpu.VMEM((m, 1), F32), pltpu.VMEM((m, 128), F32)],
        compiler_params=_cparams(("parallel", "arbitrary")),
        name="mla_prompt",
    )(q, k, wuv)


def _diff_lambda(lam_ref, lam_init):
    lp = lam_ref[...]
    a = jnp.sum(lp[0:1] * lp[1:2], axis=-1, keepdims=True)
    b = jnp.sum(lp[2:3] * lp[3:4], axis=-1, keepdims=True)
    return jnp.exp(a) - jnp.exp(b) + lam_init


def _diff_prompt_kernel(q_ref, kv_ref, lam_ref, g_ref, o_ref, m_scr, l_scr, acc_scr, *, tq, lam_init):
    qb = pl.program_id(1)
    lane = _lane_iota((tq, LANES))
    rows = []
    for h in range(DIFF_H):
        qh = q_ref[:, h * 128:(h + 1) * 128]
        rows.append(jnp.where(lane < 64, qh, jnp.zeros_like(qh)))
        rows.append(jnp.where(lane >= 64, qh, jnp.zeros_like(qh)))
    q = jnp.concatenate(rows, axis=0)
    _init_state(m_scr, l_scr, acc_scr)

    def key_of(j):
        return kv_ref[pl.ds(pl.multiple_of(j * tq, tq), tq), 0:128]

    def val_of(j):
        return kv_ref[pl.ds(pl.multiple_of(j * tq, tq), tq), 128:256]

    _causal_loop(qb, tq, q, key_of, val_of, m_scr, l_scr, acc_scr)
    o = acc_scr[...] / l_scr[...]
    lam = _diff_lambda(lam_ref, lam_init)
    for h in range(DIFF_H):
        d = o[(2 * h) * tq:(2 * h + 1) * tq] - lam * o[(2 * h + 1) * tq:(2 * h + 2) * tq]
        dn = d * lax.rsqrt(jnp.mean(d * d, axis=-1, keepdims=True) + EPS) * g_ref[...] * (1.0 - lam_init)
        o_ref[:, h * 128:(h + 1) * 128] = dn.astype(BF16)


def _diff_prompt_call(q, kv, lam_p, subln, nb, seq, tq, lam_init):
    nq = seq // tq
    m = 2 * DIFF_H * tq
    return pl.pallas_call(
        functools.partial(_diff_prompt_kernel, tq=tq, lam_init=lam_init),
        grid=(nb, nq),
        in_specs=[pl.BlockSpec((tq, 512), lambda b, i: (b * nq + i, 0)),
                  pl.BlockSpec((seq, 256), lambda b, i: (b, 0)),
                  pl.BlockSpec(lam_p.shape, lambda b, i: (0, 0)),
                  pl.BlockSpec(subln.shape, lambda b, i: (0, 0))],
        out_specs=pl.BlockSpec((tq, BRANCH_W), lambda b, i: (b * nq + i, 0)),
        out_shape=jax.ShapeDtypeStruct((nb * seq, BRANCH_W), BF16),
        scratch_shapes=[pltpu.VMEM((m, 1), F32), pltpu.VMEM((m, 1), F32), pltpu.VMEM((m, 128), F32)],
        compiler_params=_cparams(("parallel", "arbitrary")),
        name="diff_prompt",
    )(q, kv, lam_p, subln)


def _sort_key(score):
    bits = pltpu.bitcast(score + 0.0, I32)
    return bits ^ (lax.shift_right_arithmetic(bits, 31) & 0x7FFFFFFF)


def _select_threshold(count_fn, rows, topk):
    def bit_body(b, prefix):
        cand_u = prefix | lax.shift_left(jnp.int32(1), 31 - b)
        cand_s = cand_u ^ INT_MIN
        cnt = count_fn(lambda key, pos: key >= cand_s)
        return jnp.where(cnt >= topk, cand_u, prefix)

    prefix = lax.fori_loop(0, 32, bit_body, jnp.zeros((rows, 1), I32))
    t = prefix ^ INT_MIN
    c_gt = count_fn(lambda key, pos: key > t)
    c_ge = count_fn(lambda key, pos: key >= t)
    need = topk - c_gt
    tie = (c_ge > topk) & (t > KEY_NINF)
    return t, need, tie


def _tie_limit(count_fn, t, need, rows):
    def body(b, p):
        cand = p | lax.shift_left(jnp.int32(1), IDX_BITS - 1 - b)
        cnt = count_fn(lambda key, pos: (key == t) & (pos < cand))
        return jnp.where(cnt <= need, cand, p)

    return lax.fori_loop(0, IDX_BITS, body, jnp.zeros((rows, 1), I32))


def _selected(key, pos, t, plim):
    return ((key > t) | ((key == t) & (pos < plim))) & (key != KEY_NINF)


def _dsa_prompt_kernel(qi_ref, iw_ref, kv_ref, o_ref, key_scr, plim_scr, m_scr, l_scr, acc_scr, *, tq, topk):
    qb = pl.program_id(1)
    nkb = qb + 1
    lane = _lane_iota((tq, LANES))
    qs, iqs = [], []
    for h in range(DSA_H):
        v = qi_ref[:, h * 128:(h + 1) * 128]
        qs.append(jnp.where(lane < 64, v, jnp.zeros_like(v)))
        iqs.append(jnp.where(lane >= 64, v, jnp.zeros_like(v)))
    qs = jnp.concatenate(qs, axis=0)
    iqs = jnp.concatenate(iqs, axis=0)
    iw = iw_ref[...]
    col = _lane_iota((tq, tq))
    row = _row_iota((tq, tq))

    def kblock(j):
        return kv_ref[pl.ds(pl.multiple_of(j * tq, tq), tq), 0:128]

    def score_block(j, carry):
        lg = jnp.maximum(_dot_t(iqs, kblock(j)), 0.0)
        sc = iw[:, 0:1] * lg[0:tq]
        for h in range(1, IDX_H):
            sc = sc + iw[:, h:h + 1] * lg[h * tq:(h + 1) * tq]
        sc = jnp.where(col + j * tq <= row + qb * tq, sc, -jnp.inf)
        key_scr[j] = _sort_key(sc)
        return carry

    lax.fori_loop(0, nkb, score_block, 0)

    def count_fn(pred):
        def body(j, c):
            return c + jnp.where(pred(key_scr[j], col + j * tq), 1, 0)
        c = lax.fori_loop(0, nkb, body, jnp.zeros((tq, tq), I32))
        return jnp.sum(c.astype(F32), axis=-1, keepdims=True)

    t, need, tie = _select_threshold(count_fn, tq, topk)
    plim_scr[...] = jnp.full((tq, 1), 2 ** IDX_BITS, I32)

    @pl.when(jnp.max(jnp.where(tie, 1.0, 0.0)) > 0.0)
    def _():
        plim_scr[...] = _tie_limit(count_fn, t, need, tq)

    plim = plim_scr[...]
    _init_state(m_scr, l_scr, acc_scr)

    def attn_block(j, carry):
        sel = _selected(key_scr[j], col + j * tq, t, plim)
        bias = jnp.where(sel, 0.0, NEG)
        s = _dot_t(qs, kblock(j)) + jnp.concatenate([bias] * DSA_H, axis=0)
        v = kv_ref[pl.ds(pl.multiple_of(j * tq, tq), tq), 128:256]
        _softmax_step(s, v, m_scr, l_scr, acc_scr)
        return carry

    lax.fori_loop(0, nkb, attn_block, 0)
    o = acc_scr[...] / l_scr[...]
    for h in range(DSA_H):
        o_ref[:, h * DSA_D:(h + 1) * DSA_D] = o[h * tq:(h + 1) * tq, 0:DSA_D].astype(BF16)


def _dsa_prompt_call(qi, iw, kv, nb, seq, tq):
    nq = seq // tq
    m = DSA_H * tq
    return pl.pallas_call(
        functools.partial(_dsa_prompt_kernel, tq=tq, topk=float(min(IDX_TOPK, seq // 4))),
        grid=(nb, nq),
        in_specs=[pl.BlockSpec((tq, 1024), lambda b, i: (b * nq + i, 0)),
                  pl.BlockSpec((tq, IDX_H), lambda b, i: (b * nq + i, 0)),
                  pl.BlockSpec((seq, 256), lambda b, i: (b, 0))],
        out_specs=pl.BlockSpec((tq, BRANCH_W), lambda b, i: (b * nq + i, 0)),
        out_shape=jax.ShapeDtypeStruct((nb * seq, BRANCH_W), BF16),
        scratch_shapes=[pltpu.VMEM((nq, tq, tq), I32), pltpu.VMEM((tq, 1), I32),
                        pltpu.VMEM((m, 1), F32), pltpu.VMEM((m, 1), F32), pltpu.VMEM((m, 128), F32)],
        compiler_params=_cparams(("parallel", "arbitrary")),
        name="dsa_prompt",
    )(qi, iw, kv)


def _page_specs(layer, block, pages_per_step):
    def spec(i):
        return pl.BlockSpec((None, None) + block,
                            lambda b, s, pt: (layer, pt[b, s * pages_per_step + i], 0, 0))
    return [spec(i) for i in range(pages_per_step)]


def _load_pages(pages, kbuf, width):
    for i, pg in enumerate(pages):
        kbuf[i * PAGE:(i + 1) * PAGE, 0:width] = pg[...].astype(BF16)


def _load_pages_t(pages, kbuf, width):
    for i, pg in enumerate(pages):
        kbuf[0:width, i * PAGE:(i + 1) * PAGE] = pg[...].astype(BF16)


def _group_sum(x, group=8):
    n = x.shape[0] // group
    return jnp.concatenate([jnp.sum(x[i * group:(i + 1) * group], axis=0, keepdims=True) for i in range(n)], axis=0)


def _mla_sample_kernel(pt_ref, q_ref, knew_ref, wuv_ref, *rest, n_pages_step, n_steps, dec_seq):
    pages = rest[:n_pages_step]
    o_ref, kbuf, m_scr, l_scr, acc_scr = rest[n_pages_step:]
    step = pl.program_id(1)
    rows = dec_seq * MLA_H

    @pl.when(step == 0)
    def _():
        _init_state(m_scr, l_scr, acc_scr)
        kbuf[...] = jnp.zeros(kbuf.shape, BF16)

    _load_pages_t(pages, kbuf, MLA_W)
    q = q_ref[...]
    _softmax_step_t(_dot(q, kbuf[...]), kbuf[0:128, :], m_scr, l_scr, acc_scr)

    @pl.when(step == n_steps - 1)
    def _():
        qf = q.astype(F32)
        knew = knew_ref[...].astype(BF16).astype(F32)
        qrow = _row_iota((rows, 1)) // MLA_H
        cols, vrows = [], []
        for j in range(dec_seq):
            c = jnp.sum(qf[:, 0:MLA_W] * knew[j:j + 1, :], axis=-1, keepdims=True)
            cols.append(jnp.where(qrow >= j, c, NEG))
            vrows.append(knew[j:j + 1, 0:128])
        _softmax_cols(cols, vrows, m_scr, l_scr, acc_scr)
        o = (acc_scr[...] / l_scr[...]).astype(BF16)
        oall = _dot(o, wuv_ref[...])
        keep = (_lane_iota((rows, BRANCH_W)) // MLA_V) == (_row_iota((rows, BRANCH_W)) % MLA_H)
        o_ref[...] = _group_sum(jnp.where(keep, oall, 0.0), MLA_H).astype(BF16)


def _mla_sample_call(page_table, q, knew, wuv_cat, cache, layer, pps):
    nb, n_pages = page_table.shape
    n_steps = n_pages // pps
    dec_seq = knew.shape[1]
    rows = dec_seq * MLA_H
    grid_spec = pltpu.PrefetchScalarGridSpec(
        num_scalar_prefetch=1,
        grid=(nb, n_steps),
        in_specs=[pl.BlockSpec((None, rows, 256), lambda b, s, pt: (b, 0, 0)),
                  pl.BlockSpec((None, dec_seq, MLA_W), lambda b, s, pt: (b, 0, 0)),
                  pl.BlockSpec(wuv_cat.shape, lambda b, s, pt: (0, 0))] + _page_specs(layer, (MLA_W, PAGE), pps),
        out_specs=pl.BlockSpec((None, dec_seq, BRANCH_W), lambda b, s, pt: (b, 0, 0)),
        scratch_shapes=[pltpu.VMEM((256, pps * PAGE), BF16), pltpu.VMEM((rows, 1), F32),
                        pltpu.VMEM((rows, 1), F32), pltpu.VMEM((rows, 128), F32)],
    )
    return pl.pallas_call(
        functools.partial(_mla_sample_kernel, n_pages_step=pps, n_steps=n_steps, dec_seq=dec_seq),
        grid_spec=grid_spec,
        out_shape=jax.ShapeDtypeStruct((nb, dec_seq, BRANCH_W), BF16),
        compiler_params=_cparams(("parallel", "arbitrary")),
        name="mla_sample",
    )(page_table, q, knew, wuv_cat, *([cache] * pps))


def _diff_sample_kernel(pt_ref, q_ref, knew_ref, lam_ref, g_ref, *rest, n_pages_step, n_steps, dec_seq, lam_init):
    pages = rest[:n_pages_step]
    o_ref, kbuf, m_scr, l_scr, acc_scr = rest[n_pages_step:]
    step = pl.program_id(1)
    rows = dec_seq * 2 * DIFF_H

    @pl.when(step == 0)
    def _():
        _init_state(m_scr, l_scr, acc_scr)

    _load_pages(pages, kbuf, DIFF_W)
    q = q_ref[...]
    kb = kbuf[...]
    _softmax_step(_dot_t(q, kb[:, 0:128]), kb[:, 128:256], m_scr, l_scr, acc_scr)

    @pl.when(step == n_steps - 1)
    def _():
        qf = q.astype(F32)
        knew = knew_ref[...].astype(BF16).astype(F32)
        qrow = _row_iota((rows, 1)) // (2 * DIFF_H)
        cols, vrows = [], []
        for j in range(dec_seq):
            c = jnp.sum(qf * knew[j:j + 1, 0:128], axis=-1, keepdims=True)
            cols.append(jnp.where(qrow >= j, c, NEG))
            vrows.append(knew[j:j + 1, 128:256])
        _softmax_cols(cols, vrows, m_scr, l_scr, acc_scr)
        o = acc_scr[...] / l_scr[...]
        lam = _diff_lambda(lam_ref, lam_init)
        d = o - lam * pltpu.roll(o, rows - 1, 0)
        dn = d * lax.rsqrt(jnp.mean(d * d, axis=-1, keepdims=True) + EPS) * g_ref[...] * (1.0 - lam_init)
        o_ref[...] = dn.astype(BF16)


def _diff_sample_call(page_table, q, knew, lam_p, subln, cache, layer, pps, lam_init):
    nb, n_pages = page_table.shape
    n_steps = n_pages // pps
    dec_seq = knew.shape[1]
    rows = dec_seq * 2 * DIFF_H
    grid_spec = pltpu.PrefetchScalarGridSpec(
        num_scalar_prefetch=1,
        grid=(nb, n_steps),
        in_specs=[pl.BlockSpec((None, rows, 128), lambda b, s, pt: (b, 0, 0)),
                  pl.BlockSpec((None, dec_seq, DIFF_W), lambda b, s, pt: (b, 0, 0)),
                  pl.BlockSpec(lam_p.shape, lambda b, s, pt: (0, 0)),
                  pl.BlockSpec(subln.shape, lambda b, s, pt: (0, 0))] + _page_specs(layer, (PAGE, DIFF_W), pps),
        out_specs=pl.BlockSpec((None, rows, 128), lambda b, s, pt: (b, 0, 0)),
        scratch_shapes=[pltpu.VMEM((pps * PAGE, 256), BF16), pltpu.VMEM((rows, 1), F32),
                        pltpu.VMEM((rows, 1), F32), pltpu.VMEM((rows, 128), F32)],
    )
    return pl.pallas_call(
        functools.partial(_diff_sample_kernel, n_pages_step=pps, n_steps=n_steps, dec_seq=dec_seq,
                          lam_init=lam_init),
        grid_spec=grid_spec,
        out_shape=jax.ShapeDtypeStruct((nb, rows, 128), BF16),
        compiler_params=_cparams(("parallel", "arbitrary")),
        name="diff_sample",
    )(page_table, q, knew, lam_p, subln, *([cache] * pps))


def _dsa_sample_kernel(pt_ref, lhs_ref, iw_ref, knew_ref, *rest, n_pages_step, n_steps, dec_seq, topk):
    pages = rest[:n_pages_step]
    o_ref, kbuf, s_scr, v_scr, key_scr, knew_key_scr, plim_scr, m_scr, l_scr, acc_scr = rest[n_pages_step:]
    step = pl.program_id(1)
    rows = dec_seq * DSA_H
    pk = n_pages_step * PAGE
    past = n_steps * pk

    @pl.when(step == 0)
    def _():
        kbuf[...] = jnp.zeros(kbuf.shape, BF16)

    _load_pages_t(pages, kbuf, DSA_W)
    lg = _dot(lhs_ref[...], kbuf[...])
    s_scr[step] = lg[0:rows]
    v_scr[step] = kbuf[0:128, :]
    widx = jnp.maximum(lg[rows:2 * rows], 0.0) * iw_ref[...]
    key_scr[step] = _sort_key(_group_sum(widx, IDX_H))

    @pl.when(step == n_steps - 1)
    def _():
        lhs = lhs_ref[...].astype(F32)
        knew = knew_ref[...].astype(BF16).astype(F32)
        iw = iw_ref[...]
        qrow = _row_iota((rows, 1)) // DSA_H
        lane_q = _lane_iota((dec_seq, LANES))
        row_q = _row_iota((dec_seq, LANES))
        sc_new = jnp.full((dec_seq, LANES), -jnp.inf, F32)
        s_new = []
        for j in range(dec_seq):
            s_new.append(jnp.sum(lhs[0:rows, 0:DSA_D] * knew[j:j + 1, 0:DSA_D], axis=-1, keepdims=True))
            il = jnp.sum(lhs[rows:2 * rows, 2 * DSA_D:3 * DSA_D] * knew[j:j + 1, 2 * DSA_D:3 * DSA_D],
                         axis=-1, keepdims=True)
            sc = _group_sum(jnp.maximum(il, 0.0) * iw, IDX_H)
            sc_new = jnp.where((lane_q == j) & (row_q >= j), sc, sc_new)
        knew_key_scr[...] = _sort_key(sc_new)
        col = _lane_iota((dec_seq, pk))

        def count_fn(pred):
            def body(i, c):
                return c + jnp.where(pred(key_scr[i], col + i * pk), 1, 0)
            c = lax.fori_loop(0, n_steps, body, jnp.zeros((dec_seq, pk), I32))
            cn = jnp.where(pred(knew_key_scr[...], lane_q + past), 1.0, 0.0)
            return jnp.sum(c.astype(F32), axis=-1, keepdims=True) + jnp.sum(cn, axis=-1, keepdims=True)

        t, need, tie = _select_threshold(count_fn, dec_seq, topk)
        plim_scr[...] = jnp.full((dec_seq, 1), 2 ** IDX_BITS, I32)

        @pl.when(jnp.max(jnp.where(tie, 1.0, 0.0)) > 0.0)
        def _():
            plim_scr[...] = _tie_limit(count_fn, t, need, dec_seq)

        plim = plim_scr[...]
        _init_state(m_scr, l_scr, acc_scr)

        def expand(x):
            return jnp.concatenate([jnp.broadcast_to(x[i:i + 1], (DSA_H, x.shape[1])) for i in range(dec_seq)],
                                   axis=0)

        def attn_block(i, carry):
            sel = _selected(key_scr[i], col + i * pk, t, plim)
            s = s_scr[i] + expand(jnp.where(sel, 0.0, NEG))
            _softmax_step_t(s, v_scr[i], m_scr, l_scr, acc_scr)
            return carry

        lax.fori_loop(0, n_steps, attn_block, 0)
        bias_new = expand(jnp.where(_selected(knew_key_scr[...], lane_q + past, t, plim), 0.0, NEG))
        cols = [s_new[j] + bias_new[:, j:j + 1] for j in range(dec_seq)]
        vrows = [knew[j:j + 1, 0:128] for j in range(dec_seq)]
        _softmax_cols(cols, vrows, m_scr, l_scr, acc_scr)
        o_ref[...] = (acc_scr[...] / l_scr[...]).astype(BF16)


def _dsa_sample_call(page_table, lhs, iw, knew, cache, layer, pps):
    nb, n_pages = page_table.shape
    n_steps = n_pages // pps
    dec_seq = knew.shape[1]
    rows = dec_seq * DSA_H
    pk = pps * PAGE
    grid_spec = pltpu.PrefetchScalarGridSpec(
        num_scalar_prefetch=1,
        grid=(nb, n_steps),
        in_specs=[pl.BlockSpec((None, 2 * rows, 256), lambda b, s, pt: (b, 0, 0)),
                  pl.BlockSpec((None, rows, 1), lambda b, s, pt: (b, 0, 0)),
                  pl.BlockSpec((None, dec_seq, DSA_W), lambda b, s, pt: (b, 0, 0))]
        + _page_specs(layer, (DSA_W, PAGE), pps),
        out_specs=pl.BlockSpec((None, rows, 128), lambda b, s, pt: (b, 0, 0)),
        scratch_shapes=[pltpu.VMEM((256, pk), BF16),
                        pltpu.VMEM((n_steps, rows, pk), F32),
                        pltpu.VMEM((n_steps, 128, pk), BF16),
                        pltpu.VMEM((n_steps, dec_seq, pk), I32),
                        pltpu.VMEM((dec_seq, LANES), I32),
                        pltpu.VMEM((dec_seq, 1), I32),
                        pltpu.VMEM((rows, 1), F32), pltpu.VMEM((rows, 1), F32), pltpu.VMEM((rows, 128), F32)],
    )
    return pl.pallas_call(
        functools.partial(_dsa_sample_kernel, n_pages_step=pps, n_steps=n_steps, dec_seq=dec_seq,
                          topk=float(min(IDX_TOPK, (n_pages * PAGE + dec_seq) // 4))),
        grid_spec=grid_spec,
        out_shape=jax.ShapeDtypeStruct((nb, rows, 128), BF16),
        compiler_params=_cparams(("parallel", "arbitrary")),
        name="dsa_sample",
    )(page_table, lhs, iw, knew, *([cache] * pps))


def _layer_norm(z, g, b):
    mu = jnp.mean(z, axis=-1, keepdims=True)
    zc = z - mu
    var = jnp.mean(zc * zc, axis=-1, keepdims=True)
    return zc * lax.rsqrt(var + EPS) * g + b


def _merge_kernel(x_ref, om_ref, od_ref, os_ref, wg_ref, bg_ref, wb_ref, wo_ref, g1_ref, b1_ref,
                  wrh_ref, wrl_ref, br_ref, y_ref, comb_ref, *, alpha):
    tm = x_ref.shape[0]
    x = x_ref[...]
    xb = x.astype(BF16)
    mix = None
    for n, o_ref in enumerate((om_ref, od_ref, os_ref)):
        z = _dot(xb, wg_ref[:, n * D_MODEL:(n + 1) * D_MODEL]) + bg_ref[:, n * D_MODEL:(n + 1) * D_MODEL]
        gate = 1.0 / (1.0 + jnp.exp(-z))
        term = gate * _dot(o_ref[...], wb_ref[n])
        mix = term if mix is None else mix + term
    mixed = _dot(mix.astype(BF16), wo_ref[...])
    y = _layer_norm(alpha * x + mixed, g1_ref[...], b1_ref[...])
    y_ref[...] = y

    yh = y.astype(BF16)
    yl = (y - yh.astype(F32)).astype(BF16)
    logit = _dot(yh, wrh_ref[...]) + (_dot(yh, wrl_ref[...]) + _dot(yl, wrh_ref[...])) + br_ref[...]
    lane_i = _lane_iota((tm, LANES))
    lane = lane_i.astype(F32)
    big = float(LANES)
    gl = jnp.where(lane_i < N_GROUPS, logit, -jnp.inf)
    gmax = jnp.max(gl, axis=-1, keepdims=True)
    pg = jnp.exp(gl - gmax)
    pg = pg / jnp.sum(pg, axis=-1, keepdims=True)
    pg_star = jnp.max(pg, axis=-1, keepdims=True)
    g_star = jnp.min(jnp.where(pg == pg_star, lane, big), axis=-1, keepdims=True)
    group_of = lax.shift_right_arithmetic(lane_i - N_GROUPS, 3).astype(F32)
    in_group = (lane_i >= N_GROUPS) & (group_of == g_star)
    el = jnp.where(in_group, logit, -jnp.inf)
    pe = jnp.exp(el - jnp.max(el, axis=-1, keepdims=True))
    pe = pe / jnp.sum(pe, axis=-1, keepdims=True)
    pe = jnp.where(in_group, pe, -1.0)
    w1 = jnp.max(pe, axis=-1, keepdims=True)
    i1 = jnp.min(jnp.where(pe == w1, lane, big), axis=-1, keepdims=True)
    pe2 = jnp.where(lane == i1, -1.0, pe)
    w2 = jnp.max(pe2, axis=-1, keepdims=True)
    i2 = jnp.min(jnp.where(pe2 == w2, lane, big), axis=-1, keepdims=True)
    wsum = w1 + w2
    comb_ref[...] = jnp.where(lane == i1, pg_star * w1 / wsum, 0.0) + jnp.where(lane == i2, pg_star * w2 / wsum, 0.0)


def _merge_call(x, om, od, os_, wg, bg, wb, wo, g1, b1, wrh, wrl, br, tm, alpha):
    t = x.shape[0]
    row = lambda w: pl.BlockSpec((tm, w), lambda i: (i, 0))
    full = lambda a: pl.BlockSpec(a.shape, lambda i: (0,) * a.ndim)
    return pl.pallas_call(
        functools.partial(_merge_kernel, alpha=alpha),
        grid=(t // tm,),
        in_specs=[row(D_MODEL), row(BRANCH_W), row(BRANCH_W), row(BRANCH_W), full(wg), full(bg), full(wb),
                  full(wo), full(g1), full(b1), full(wrh), full(wrl), full(br)],
        out_specs=[row(D_MODEL), row(LANES)],
        out_shape=[jax.ShapeDtypeStruct((t, D_MODEL), F32), jax.ShapeDtypeStruct((t, LANES), F32)],
        compiler_params=_cparams(("parallel",)),
        name="merge",
    )(x, om, od, os_, wg, bg, wb, wo, g1, b1, wrh, wrl, br)


def _moe_kernel(y_ref, comb_ref, win_ref, wout_ref, g2_ref, b2_ref, o_ref, xb_scr, acc_scr, *, alpha):
    e = pl.program_id(1)
    tm = y_ref.shape[0]

    @pl.when(e == 0)
    def _():
        xb_scr[...] = y_ref[...].astype(BF16)
        acc_scr[...] = jnp.zeros(acc_scr.shape, F32)

    h = _dot(xb_scr[...], win_ref[...])
    lane = _lane_iota((tm, LANES))
    c = jnp.sum(jnp.where(lane == e + N_GROUPS, comb_ref[...], 0.0), axis=-1, keepdims=True)
    h1 = h[:, 0:D_FF]
    act = h1 / (1.0 + jnp.exp(-h1)) * h[:, D_FF:2 * D_FF] * c
    acc_scr[...] += _dot(act.astype(BF16), wout_ref[...])

    @pl.when(e == N_EXPERTS - 1)
    def _():
        o_ref[...] = _layer_norm(alpha * y_ref[...] + acc_scr[...], g2_ref[...], b2_ref[...])


def _moe_call(y, comb, win, wout, g2, b2, tm, alpha):
    t = y.shape[0]
    return pl.pallas_call(
        functools.partial(_moe_kernel, alpha=alpha),
        grid=(t // tm, N_EXPERTS),
        in_specs=[pl.BlockSpec((tm, D_MODEL), lambda i, e: (i, 0)),
                  pl.BlockSpec((tm, LANES), lambda i, e: (i, 0)),
                  pl.BlockSpec((None, D_MODEL, 2 * D_FF), lambda i, e: (e, 0, 0)),
                  pl.BlockSpec((None, D_FF, D_MODEL), lambda i, e: (e, 0, 0)),
                  pl.BlockSpec(g2.shape, lambda i, e: (0, 0)),
                  pl.BlockSpec(b2.shape, lambda i, e: (0, 0))],
        out_specs=pl.BlockSpec((tm, D_MODEL), lambda i, e: (i, 0)),
        out_shape=jax.ShapeDtypeStruct((t, D_MODEL), F32),
        scratch_shapes=[pltpu.VMEM((tm, D_MODEL), BF16), pltpu.VMEM((tm, D_MODEL), F32)],
        compiler_params=_cparams(("parallel", "arbitrary")),
        name="moe",
    )(y, comb, win, wout, g2, b2)


def _in_proj_perm():
    widths = (MLA_Q_LORA, MLA_KV_LORA, MLA_ROPE, DIFF_H * 2 * DIFF_D, 2 * DIFF_D, 2 * DIFF_D,
              DSA_H * DSA_D, DSA_D, DSA_D, IDX_H * IDX_D, IDX_D, IDX_H)
    off = np.concatenate([[0], np.cumsum(widths)])
    c_q, c_kv, kr, dq, dk, dv, sq, sk, sv, iq, ik, iw = [np.arange(off[i], off[i + 1]) for i in range(12)]
    qi = np.concatenate([np.concatenate([sq[h * 64:(h + 1) * 64], iq[h * 64:(h + 1) * 64]]) for h in range(DSA_H)])
    pad = -np.ones(64 - MLA_ROPE - IDX_H, np.int64)
    perm = np.concatenate([c_q, dq, qi, dk, dv, sk, ik, kr, iw, pad, sv, c_kv])
    assert perm.shape[0] == IN_COLS
    scale = np.ones(IN_COLS, np.float32)
    scale[C_DQ:C_DQ + 512] = DIFF_SCALE
    for h in range(DSA_H):
        scale[C_QI + h * 128:C_QI + h * 128 + 64] = DSA_SCALE
    return perm, scale


def _rope_tables(pos):
    posf = pos.astype(F32)[:, None]
    inv64 = 1.0 / (ROPE_THETA ** (jnp.arange(32, dtype=F32) / 32))
    a64 = posf * inv64[None, :]
    c64, s64 = jnp.cos(a64), jnp.sin(a64)
    cos64 = jnp.tile(jnp.concatenate([c64, c64], axis=-1), (1, 2))
    sin64 = jnp.tile(jnp.concatenate([-s64, s64], axis=-1), (1, 2))
    inv32 = 1.0 / (ROPE_THETA ** (jnp.arange(16, dtype=F32) / 16))
    a32 = posf * inv32[None, :]
    c32, s32 = jnp.cos(a32), jnp.sin(a32)
    n = pos.shape[0]
    cosh = jnp.concatenate([c32, c32, jnp.ones((n, 96), F32)], axis=-1)
    sinh = jnp.concatenate([-s32, s32, jnp.zeros((n, 96), F32)], axis=-1)
    return cos64, sin64, cosh, sinh


def _pick_tile(t, candidates):
    for c in candidates:
        if t % c == 0:
            return c
    raise ValueError(f"no tile for {t}")


def kernel(x_prompt, x_sample, cache_mla, cache_diff, cache_dsa, page_table, w_in, mla_gq, mla_gkv, mla_wuq,
           mla_wukv, diff_lam, diff_subln, w_gate, b_gate, w_branch, w_o, ln1_g, ln1_b, w_rg, b_rg, w_re, b_re,
           w_e_in, w_e_out, ln2_g, ln2_b):
    nb, seq, d = x_prompt.shape
    db, dec_seq, _ = x_sample.shape
    depth = w_in.shape[0]
    n_pages = page_table.shape[1]
    past_len = n_pages * PAGE
    tp, ts = nb * seq, db * dec_seq
    t = tp + ts
    alpha = (2 * depth) ** 0.25
    assert d == D_MODEL and seq % 128 == 0

    tq = _pick_tile(seq, (256, 128))
    tm = _pick_tile(t, (512, 256, 128, 64, 32, 16, 8))
    tm_moe = _pick_tile(t, (1536, 1024, 768, 512, 256, 128, 64, 32, 16, 8))
    pps = _pick_tile(n_pages, (16, 8, 4, 2, 1))

    pos = jnp.concatenate([jnp.tile(jnp.arange(seq, dtype=I32), nb),
                           jnp.tile(past_len + jnp.arange(dec_seq, dtype=I32), db)])
    cos64, sin64, cosh, sinh = _rope_tables(pos)
    perm, col_scale = _in_proj_perm()
    x = jnp.concatenate([x_prompt.reshape(tp, d), x_sample.reshape(ts, d)], axis=0)
    outs = [[] for _ in range(6)]
    cache_mla_t = jnp.swapaxes(cache_mla, 2, 3)
    cache_dsa_t = jnp.swapaxes(cache_dsa, 2, 3)

    for l in range(depth):
        lam_init = 0.8 - 0.6 * math.exp(-0.3 * l)
        runs = np.flatnonzero((np.diff(perm) != 1) & ~((perm[1:] < 0) & (perm[:-1] < 0))) + 1
        pieces = []
        for a, b in zip(np.concatenate([[0], runs]), np.concatenate([runs, [IN_COLS]])):
            pieces.append(jnp.zeros((d, int(b - a)), F32) if perm[a] < 0
                          else w_in[l][:, int(perm[a]):int(perm[a]) + int(b - a)])
        w_in_p = (jnp.concatenate(pieces, axis=1) * jnp.asarray(col_scale)[None, :]).astype(BF16)
        wuq = mla_wuq[l].reshape(MLA_Q_LORA, MLA_H, MLA_NOPE + MLA_ROPE)
        wukv = mla_wukv[l].reshape(MLA_KV_LORA, MLA_H, MLA_NOPE + MLA_V)
        w_abs = jnp.einsum('rhn,chn->rhc', wuq[..., :MLA_NOPE], wukv[..., :MLA_NOPE],
                           precision=lax.Precision.HIGHEST)
        wq = jnp.concatenate([w_abs, wuq[..., MLA_NOPE:], jnp.zeros((MLA_Q_LORA, MLA_H, 96), F32)], axis=-1)
        wq = (wq * MLA_SCALE).reshape(MLA_Q_LORA, MLA_H * 256).astype(BF16)
        wuv = jnp.transpose(wukv[..., MLA_NOPE:], (1, 0, 2)).astype(BF16)
        wuv_cat = wukv[..., MLA_NOPE:].reshape(MLA_KV_LORA, MLA_H * MLA_V).astype(BF16)
        wr = jnp.concatenate([w_rg[l], w_re[l], jnp.zeros((d, LANES - N_GROUPS - N_EXPERTS), F32)], axis=1)
        wrh = wr.astype(BF16)
        wrl = (wr - wrh.astype(F32)).astype(BF16)
        br = jnp.concatenate([b_rg[l], b_re[l], jnp.zeros((LANES - N_GROUPS - N_EXPERTS,), F32)])[None, :]

        (nmla, ndiff, ndsa, qmla, kmla, qdiff, kvdiff, qidsa, kvdsa, iw) = _proj_call(
            x, w_in_p, wq, mla_gq[l][None, :], mla_gkv[l][None, :], cos64, sin64, cosh, sinh, tm)

        om_p = _mla_prompt_call(qmla, kmla, wuv, nb, seq, tq)
        od_p = _diff_prompt_call(qdiff, kvdiff, diff_lam[l], diff_subln[l][None, :], nb, seq, tq, lam_init)
        os_p = _dsa_prompt_call(qidsa, iw, kvdsa, nb, seq, 128)

        q_s = qmla[tp:].reshape(db, dec_seq * MLA_H, 256)
        om_s = _mla_sample_call(page_table, q_s, nmla[tp:].reshape(db, dec_seq, MLA_W), wuv_cat, cache_mla_t, l, pps)
        qd = qdiff[tp:].reshape(db, dec_seq, DIFF_H, 2, DIFF_D)
        zd = jnp.zeros_like(qd[..., 0, :])
        qd = jnp.stack([jnp.concatenate([qd[..., 0, :], zd], axis=-1),
                        jnp.concatenate([zd, qd[..., 1, :]], axis=-1)], axis=3)
        od_s = _diff_sample_call(page_table, qd.reshape(db, dec_seq * 2 * DIFF_H, 128),
                                 ndiff[tp:].reshape(db, dec_seq, DIFF_W), diff_lam[l], diff_subln[l][None, :],
                                 cache_diff, l, pps, lam_init)
        od_s = od_s.reshape(db, dec_seq, DIFF_H, 2, 128)[:, :, :, 0, :]
        qi = qidsa[tp:].reshape(db, dec_seq * DSA_H, 128)
        z64 = jnp.zeros_like(qi[..., :64])
        lhs = jnp.concatenate([jnp.concatenate([qi[..., :64], z64, z64, z64], axis=-1),
                               jnp.concatenate([z64, z64, qi[..., 64:], z64], axis=-1)], axis=1)
        os_s = _dsa_sample_call(page_table, lhs, iw[tp:].reshape(db, dec_seq * IDX_H, 1),
                                ndsa[tp:].reshape(db, dec_seq, DSA_W), cache_dsa_t, l, pps)
        os_s = os_s.reshape(db, dec_seq, DSA_H, 128)[..., 64:]

        om = jnp.concatenate([om_p, om_s.reshape(ts, BRANCH_W)], axis=0)
        od = jnp.concatenate([od_p, od_s.reshape(ts, BRANCH_W)], axis=0)
        os_ = jnp.concatenate([os_p, os_s.reshape(ts, BRANCH_W)], axis=0)

        y, comb = _merge_call(x, om, od, os_, w_gate[l].astype(BF16), b_gate[l][None, :], w_branch[l].astype(BF16),
                              w_o[l].astype(BF16), ln1_g[l][None, :], ln1_b[l][None, :], wrh, wrl, br, tm, alpha)
        x = _moe_call(y, comb, w_e_in[l].astype(BF16), w_e_out[l].astype(BF16), ln2_g[l][None, :],
                      ln2_b[l][None, :], tm_moe, alpha)

        for lst, arr, w in ((outs[0], nmla, MLA_W), (outs[2], ndiff, DIFF_W), (outs[4], ndsa, DSA_W)):
            lst.append(arr[:tp].reshape(nb, seq, w))
        for lst, arr, w in ((outs[1], nmla, MLA_W), (outs[3], ndiff, DIFF_W), (outs[5], ndsa, DSA_W)):
            lst.append(arr[tp:].reshape(db, dec_seq, w))

    return (x[:tp].reshape(nb, seq, d), x[tp:].reshape(db, dec_seq, d),
            jnp.stack(outs[0]), jnp.stack(outs[1]), jnp.stack(outs[2]), jnp.stack(outs[3]),
            jnp.stack(outs[4]), jnp.stack(outs[5]))
```

```python
import functools
import math

import numpy as np
import jax
import jax.numpy as jnp
from jax import lax
from jax.experimental import pallas as pl
from jax.experimental.pallas import tpu as pltpu

F32 = jnp.float32
BF16 = jnp.bfloat16
I32 = jnp.int32

D_MODEL = 1024
PAGE = 128
MLA_H, MLA_Q_LORA, MLA_KV_LORA, MLA_NOPE, MLA_ROPE, MLA_V = 8, 256, 128, 64, 32, 64
MLA_SCALE = (MLA_NOPE + MLA_ROPE) ** -0.5
DIFF_H, DIFF_D = 4, 64
DIFF_SCALE = DIFF_D ** -0.5
DSA_H, DSA_D = 8, 64
DSA_SCALE = DSA_D ** -0.5
IDX_H, IDX_D, IDX_TOPK = 8, 64, 256
IDX_W_SCALE = (IDX_H ** -0.5) * (IDX_D ** -0.5)
N_BRANCH, BRANCH_W = 3, 512
N_GROUPS, EXPERTS_PER_GROUP, TOP_E, D_FF = 4, 8, 2, 256
N_EXPERTS = N_GROUPS * EXPERTS_PER_GROUP
ROPE_THETA = 10000.0
EPS = 1e-5
MLA_W, DIFF_W, DSA_W = 160, 256, 192

LANES = 128
SUBLANES = 8
NEG = -1e30
INT_MIN = -(2 ** 31)
KEY_NINF = -2139095041
IDX_BITS = 15
VMEM_LIMIT = 52 * 1024 * 1024

R_CQ, R_DQ, R_QI, R_DK, R_DV, R_G2, R_CKV, IN_ROWS = 0, 256, 768, 1792, 1920, 2048, 2304, 2432
TOK_TILE = 512


def _cparams(sem):
    return pltpu.CompilerParams(dimension_semantics=sem, vmem_limit_bytes=VMEM_LIMIT)


def _dot(a, b):
    return jnp.dot(a, b, preferred_element_type=F32)


def _dot_t(a, b):
    return lax.dot_general(a, b, (((1,), (1,)), ((), ())), preferred_element_type=F32)


def _lane_iota(shape):
    return lax.broadcasted_iota(I32, shape, len(shape) - 1)


def _row_iota(shape):
    return lax.broadcasted_iota(I32, shape, 0)


def _rope_rows(v, c, s):
    half = v.shape[0] // 2
    x1, x2 = v[:half], v[half:]
    return jnp.concatenate([x1 * c - x2 * s, x2 * c + x1 * s], axis=0)


def _rms_rows(v, g):
    return v * lax.rsqrt(jnp.mean(v * v, axis=0, keepdims=True) + EPS) * g


def _proj_kernel(x_ref, wt_ref, wqt_ref, gq_ref, gkv_ref, c64_ref, s64_ref, c32_ref, s32_ref,
                 nmla_t_ref, ndiff_ref, ndsa_t_ref, qmla_t_ref, kmla_ref, vmla_t_ref, qdiff_t_ref, kdiff_ref,
                 vdiff_t_ref, qidsa_t_ref, kdsa_ref, vdsa_t_ref, iw_t_ref):
    tm = x_ref.shape[0]
    xb = x_ref[...].astype(BF16)
    c64, s64 = c64_ref[...], s64_ref[...]
    c32, s32 = c32_ref[...], s32_ref[...]

    def seg(lo, hi):
        return _dot_t(wt_ref[lo:hi, :], xb)

    cqn = _rms_rows(seg(R_CQ, R_CQ + 256), gq_ref[...]).astype(BF16)
    for h in range(MLA_H):
        qh = _dot(wqt_ref[h * 256:(h + 1) * 256, :], cqn)
        qmla_t_ref[h * 256:h * 256 + 128, :] = qh[0:128].astype(BF16)
        qmla_t_ref[h * 256 + 128:h * 256 + 160, :] = _rope_rows(qh[128:160], c32, s32).astype(BF16)
        qmla_t_ref[h * 256 + 160:(h + 1) * 256, :] = jnp.zeros((96, tm), BF16)

    dq = seg(R_DQ, R_DQ + 512)
    for c in range(8):
        qdiff_t_ref[c * 64:(c + 1) * 64, :] = _rope_rows(dq[c * 64:(c + 1) * 64], c64, s64).astype(BF16)

    for half in range(2):
        qi = seg(R_QI + half * 512, R_QI + (half + 1) * 512)
        for c in range(8):
            r0 = half * 512 + c * 64
            qidsa_t_ref[r0:r0 + 64, :] = _rope_rows(qi[c * 64:(c + 1) * 64], c64, s64).astype(BF16)

    dk = seg(R_DK, R_DK + 128)
    dk = jnp.concatenate([_rope_rows(dk[0:64], c64, s64), _rope_rows(dk[64:128], c64, s64)], axis=0)
    dv = seg(R_DV, R_DV + 128)
    dk_r = dk.T
    ndiff_ref[:, 0:128] = dk_r
    ndiff_ref[:, 128:256] = dv.T
    kdiff_ref[...] = dk_r.astype(BF16)
    vdiff_t_ref[...] = dv.astype(BF16)

    g2 = seg(R_G2, R_G2 + 256)
    sk = _rope_rows(g2[0:64], c64, s64)
    ik = _rope_rows(g2[64:128], c64, s64)
    sv = g2[128:192]
    kr = _rope_rows(g2[192:224], c32, s32)
    ndsa_t_ref[0:64, :] = sk
    ndsa_t_ref[64:128, :] = sv
    ndsa_t_ref[128:192, :] = ik
    kdsa_ref[...] = jnp.concatenate([sk, ik], axis=0).T.astype(BF16)
    vdsa_t_ref[...] = sv.astype(BF16)
    iw_t_ref[...] = g2[224:232] * IDX_W_SCALE

    ckvn = _rms_rows(seg(R_CKV, R_CKV + 128), gkv_ref[...])
    nmla_t_ref[0:128, :] = ckvn
    nmla_t_ref[128:160, :] = kr
    vmla_t_ref[...] = ckvn.astype(BF16)
    kmla_ref[:, 0:128] = ckvn.T.astype(BF16)
    kmla_ref[:, 128:256] = jnp.concatenate([kr, jnp.zeros((96, tm), F32)], axis=0).T.astype(BF16)


def _proj_call(x, w_in_t, wq_t, gq, gkv, c64, s64, c32, s32):
    t = x.shape[0]
    tm = TOK_TILE
    nt = t // tm
    row = lambda w: pl.BlockSpec((tm, w), lambda i: (i, 0))
    col = lambda r: pl.BlockSpec((r, tm), lambda i: (0, i))
    chunk = lambda r: pl.BlockSpec((None, r, tm), lambda i: (i, 0, 0))
    full = lambda a: pl.BlockSpec(a.shape, lambda i: (0,) * a.ndim)
    sds = jax.ShapeDtypeStruct
    out_shapes = [sds((MLA_W, t), F32), sds((t, DIFF_W), F32), sds((DSA_W, t), F32),
                  sds((2048, t), BF16), sds((t, 256), BF16), sds((nt, 128, tm), BF16),
                  sds((512, t), BF16), sds((t, 128), BF16), sds((nt, 128, tm), BF16),
                  sds((1024, t), BF16), sds((t, 128), BF16), sds((nt, 64, tm), BF16),
                  sds((IDX_H, t), F32)]
    out_specs = [col(MLA_W), row(DIFF_W), col(DSA_W),
                 col(2048), row(256), chunk(128),
                 col(512), row(128), chunk(128),
                 col(1024), row(128), chunk(64),
                 col(IDX_H)]
    return pl.pallas_call(
        _proj_kernel,
        grid=(nt,),
        in_specs=[row(D_MODEL), full(w_in_t), full(wq_t), full(gq), full(gkv), col(32), col(32), col(16), col(16)],
        out_specs=out_specs,
        out_shape=out_shapes,
        compiler_params=_cparams(("parallel",)),
        name="proj_in",
    )(x, w_in_t, wq_t, gq, gkv, c64, s64, c32, s32)


def _softmax_step_rows(s_t, v_t, m_scr, l_scr, acc_scr):
    m_old = m_scr[...]
    m_new = jnp.maximum(m_old, jnp.max(s_t, axis=0, keepdims=True))
    alpha = jnp.exp(m_old - m_new)
    p_t = jnp.exp(s_t - m_new)
    l_scr[...] = alpha * l_scr[...] + jnp.sum(p_t, axis=0, keepdims=True)
    acc_scr[...] = alpha * acc_scr[...] + _dot(v_t, p_t.astype(BF16))
    m_scr[...] = m_new


def _softmax_step(s, v, m_scr, l_scr, acc_scr):
    m_old = m_scr[...]
    m_new = jnp.maximum(m_old, jnp.max(s, axis=-1, keepdims=True))
    alpha = jnp.exp(m_old - m_new)
    p = jnp.exp(s - m_new)
    l_scr[...] = alpha * l_scr[...] + jnp.sum(p, axis=-1, keepdims=True)
    acc_scr[...] = alpha * acc_scr[...] + _dot(p.astype(BF16), v)
    m_scr[...] = m_new


def _softmax_step_t(s, vt, m_scr, l_scr, acc_scr):
    m_old = m_scr[...]
    m_new = jnp.maximum(m_old, jnp.max(s, axis=-1, keepdims=True))
    alpha = jnp.exp(m_old - m_new)
    p = jnp.exp(s - m_new)
    l_scr[...] = alpha * l_scr[...] + jnp.sum(p, axis=-1, keepdims=True)
    acc_scr[...] = alpha * acc_scr[...] + _dot_t(p.astype(BF16), vt)
    m_scr[...] = m_new


def _softmax_cols(cols, vrows, m_scr, l_scr, acc_scr):
    m_old = m_scr[...]
    m_new = m_old
    for c in cols:
        m_new = jnp.maximum(m_new, c)
    alpha = jnp.exp(m_old - m_new)
    l_new = alpha * l_scr[...]
    acc = alpha * acc_scr[...]
    for c, v in zip(cols, vrows):
        p = jnp.exp(c - m_new)
        l_new = l_new + p
        acc = acc + p * v
    l_scr[...] = l_new
    acc_scr[...] = acc
    m_scr[...] = m_new


def _init_state(m_scr, l_scr, acc_scr):
    m_scr[...] = jnp.full(m_scr.shape, NEG, F32)
    l_scr[...] = jnp.zeros(l_scr.shape, F32)
    acc_scr[...] = jnp.zeros(acc_scr.shape, F32)


def _causal_chunks(qb, tq, tk, q_t, k_ref, vt_ref, m_scr, l_scr, acc_scr):
    m = q_t.shape[1]
    n_full = (qb * tq) // tk

    def full_chunk(j, carry):
        kb = k_ref[pl.ds(pl.multiple_of(j * tk, tk), tk), :]
        _softmax_step_rows(_dot(kb, q_t), vt_ref[j], m_scr, l_scr, acc_scr)
        return carry

    lax.fori_loop(0, n_full, full_chunk, 0)
    kb = k_ref[pl.ds(pl.multiple_of(n_full * tk, tk), tk), :]
    s_t = _dot(kb, q_t)
    key_pos = _row_iota((tk, m)) + n_full * tk
    q_pos = (_lane_iota((tk, m)) % tq) + qb * tq
    _softmax_step_rows(jnp.where(key_pos <= q_pos, s_t, NEG), vt_ref[n_full], m_scr, l_scr, acc_scr)


def _mla_prompt_kernel(qt_ref, k_ref, vt_ref, wuvt_ref, o_ref, m_scr, l_scr, acc_scr, *, tq, tk):
    qb = pl.program_id(1)
    q_t = jnp.concatenate([qt_ref[h * 256:(h + 1) * 256, :] for h in range(MLA_H)], axis=1)
    _init_state(m_scr, l_scr, acc_scr)
    _causal_chunks(qb, tq, tk, q_t, k_ref, vt_ref, m_scr, l_scr, acc_scr)
    o_t = (acc_scr[...] / l_scr[...]).astype(BF16)
    heads = [_dot(wuvt_ref[h], o_t[:, h * tq:(h + 1) * tq]) for h in range(MLA_H)]
    o_ref[...] = jnp.concatenate(heads, axis=0).T.astype(BF16)


def _mla_prompt_call(q_t, k, v_t, wuv_t, nb, seq, tq):
    tk = TOK_TILE
    nq = seq // tq
    m = MLA_H * tq
    return pl.pallas_call(
        functools.partial(_mla_prompt_kernel, tq=tq, tk=tk),
        grid=(nb, nq),
        in_specs=[pl.BlockSpec((2048, tq), lambda b, i: (0, b * nq + i)),
                  pl.BlockSpec((seq, 256), lambda b, i: (b, 0)),
                  pl.BlockSpec((seq // tk, 128, tk), lambda b, i: (b, 0, 0)),
                  pl.BlockSpec(wuv_t.shape, lambda b, i: (0, 0, 0))],
        out_specs=pl.BlockSpec((tq, BRANCH_W), lambda b, i: (b * nq + i, 0)),
        out_shape=jax.ShapeDtypeStruct((nb * seq, BRANCH_W), BF16),
        scratch_shapes=[pltpu.VMEM((1, m), F32), pltpu.VMEM((1, m), F32), pltpu.VMEM((128, m), F32)],
        compiler_params=_cparams(("parallel", "arbitrary")),
        name="mla_prompt",
    )(q_t, k, v_t, wuv_t)


def _diff_lambda(lam_ref, lam_init):
    lp = lam_ref[...]
    a = jnp.sum(lp[0:1] * lp[1:2], axis=-1, keepdims=True)
    b = jnp.sum(lp[2:3] * lp[3:4], axis=-1, keepdims=True)
    return jnp.exp(a) - jnp.exp(b) + lam_init


def _diff_prompt_kernel(qt_ref, k_ref, vt_ref, lam_ref, g_ref, o_ref, m_scr, l_scr, acc_scr, *, tq, tk, lam_init):
    qb = pl.program_id(1)
    zero = jnp.zeros((DIFF_D, tq), BF16)
    cols = []
    for h in range(DIFF_H):
        qh = qt_ref[h * 128:(h + 1) * 128, :]
        cols.append(jnp.concatenate([qh[0:64], zero], axis=0))
        cols.append(jnp.concatenate([zero, qh[64:128]], axis=0))
    q_t = jnp.concatenate(cols, axis=1)
    _init_state(m_scr, l_scr, acc_scr)
    _causal_chunks(qb, tq, tk, q_t, k_ref, vt_ref, m_scr, l_scr, acc_scr)
    o_t = acc_scr[...] / l_scr[...]
    lam = _diff_lambda(lam_ref, lam_init)
    outs = []
    for h in range(DIFF_H):
        d = o_t[:, (2 * h) * tq:(2 * h + 1) * tq] - lam * o_t[:, (2 * h + 1) * tq:(2 * h + 2) * tq]
        outs.append(_rms_rows(d, g_ref[...]) * (1.0 - lam_init))
    o_ref[...] = jnp.concatenate(outs, axis=0).T.astype(BF16)


def _diff_prompt_call(q_t, k, v_t, lam_p, subln_col, nb, seq, tq, lam_init):
    tk = TOK_TILE
    nq = seq // tq
    m = 2 * DIFF_H * tq
    return pl.pallas_call(
        functools.partial(_diff_prompt_kernel, tq=tq, tk=tk, lam_init=lam_init),
        grid=(nb, nq),
        in_specs=[pl.BlockSpec((512, tq), lambda b, i: (0, b * nq + i)),
                  pl.BlockSpec((seq, 128), lambda b, i: (b, 0)),
                  pl.BlockSpec((seq // tk, 128, tk), lambda b, i: (b, 0, 0)),
                  pl.BlockSpec(lam_p.shape, lambda b, i: (0, 0)),
                  pl.BlockSpec(subln_col.shape, lambda b, i: (0, 0))],
        out_specs=pl.BlockSpec((tq, BRANCH_W), lambda b, i: (b * nq + i, 0)),
        out_shape=jax.ShapeDtypeStruct((nb * seq, BRANCH_W), BF16),
        scratch_shapes=[pltpu.VMEM((1, m), F32), pltpu.VMEM((1, m), F32), pltpu.VMEM((128, m), F32)],
        compiler_params=_cparams(("parallel", "arbitrary")),
        name="diff_prompt",
    )(q_t, k, v_t, lam_p, subln_col)


def _sort_key(score):
    bits = pltpu.bitcast(score + 0.0, I32)
    return bits ^ (lax.shift_right_arithmetic(bits, 31) & 0x7FFFFFFF)


def _select_threshold(count_fn, shape, topk):
    def bit_body(b, prefix):
        cand_u = prefix | lax.shift_left(jnp.int32(1), 31 - b)
        cand_s = cand_u ^ INT_MIN
        cnt = count_fn(lambda key, pos: key >= cand_s)
        return jnp.where(cnt >= topk, cand_u, prefix)

    prefix = lax.fori_loop(0, 32, bit_body, jnp.zeros(shape, I32))
    t = prefix ^ INT_MIN
    c_gt = count_fn(lambda key, pos: key > t)
    c_ge = count_fn(lambda key, pos: key >= t)
    need = topk - c_gt
    tie = (c_ge > topk) & (t > KEY_NINF)
    return t, need, tie


def _tie_limit(count_fn, t, need, shape):
    def body(b, p):
        cand = p | lax.shift_left(jnp.int32(1), IDX_BITS - 1 - b)
        cnt = count_fn(lambda key, pos: (key == t) & (pos < cand))
        return jnp.where(cnt <= need, cand, p)

    return lax.fori_loop(0, IDX_BITS, body, jnp.zeros(shape, I32))


def _selected(key, pos, t, plim):
    return ((key > t) | ((key == t) & (pos < plim))) & (key != KEY_NINF)


def _dsa_prompt_kernel(qit_ref, iwt_ref, k_ref, vt_ref, o_ref, key_scr, plim_scr, m_scr, l_scr, acc_scr,
                       *, tq, tk, topk):
    qb = pl.program_id(1)
    n_chunks = (qb * tq) // tk + 1
    zero = jnp.zeros((DSA_D, tq), BF16)
    qs, iqs = [], []
    for h in range(DSA_H):
        slab = qit_ref[h * 128:(h + 1) * 128, :]
        qs.append(jnp.concatenate([slab[0:64], zero], axis=0))
        iqs.append(jnp.concatenate([zero, slab[64:128]], axis=0))
    qs_t = jnp.concatenate(qs, axis=1)
    iqs_t = jnp.concatenate(iqs, axis=1)
    iw_t = iwt_ref[...]
    row = _row_iota((tk, tq))
    q_pos = _lane_iota((tk, tq)) + qb * tq

    def kchunk(j):
        return k_ref[pl.ds(pl.multiple_of(j * tk, tk), tk), :]

    def score_chunk(j, carry):
        lg = jnp.maximum(_dot(kchunk(j), iqs_t), 0.0)
        sc = iw_t[0:1, :] * lg[:, 0:tq]
        for h in range(1, IDX_H):
            sc = sc + iw_t[h:h + 1, :] * lg[:, h * tq:(h + 1) * tq]
        sc = jnp.where(row + j * tk <= q_pos, sc, -jnp.inf)
        key_scr[j] = _sort_key(sc)
        return carry

    lax.fori_loop(0, n_chunks, score_chunk, 0)

    def count_fn(pred):
        def body(j, c):
            hit = jnp.where(pred(key_scr[j], row + j * tk), 1, 0)
            return c + jnp.sum(hit.reshape(tk // SUBLANES, SUBLANES, tq), axis=0)
        c = lax.fori_loop(0, n_chunks, body, jnp.zeros((SUBLANES, tq), I32))
        return jnp.sum(c.astype(F32), axis=0, keepdims=True)

    t, need, tie = _select_threshold(count_fn, (1, tq), topk)
    plim_scr[...] = jnp.full((1, tq), 2 ** IDX_BITS, I32)

    @pl.when(jnp.max(jnp.where(tie, 1.0, 0.0)) > 0.0)
    def _():
        plim_scr[...] = _tie_limit(count_fn, t, need, (1, tq))

    plim = plim_scr[...]
    _init_state(m_scr, l_scr, acc_scr)

    def attn_chunk(j, carry):
        sel = _selected(key_scr[j], row + j * tk, t, plim)
        bias = jnp.where(sel, 0.0, NEG)
        s_t = _dot(kchunk(j), qs_t) + jnp.concatenate([bias] * DSA_H, axis=1)
        _softmax_step_rows(s_t, vt_ref[j], m_scr, l_scr, acc_scr)
        return carry

    lax.fori_loop(0, n_chunks, attn_chunk, 0)
    o_t = acc_scr[...] / l_scr[...]
    o_ref[...] = jnp.concatenate([o_t[:, h * tq:(h + 1) * tq] for h in range(DSA_H)], axis=0).T.astype(BF16)


def _dsa_prompt_call(qi_t, iw_t, k, v_t, nb, seq, tq):
    tk = TOK_TILE
    nq = seq // tq
    m = DSA_H * tq
    return pl.pallas_call(
        functools.partial(_dsa_prompt_kernel, tq=tq, tk=tk, topk=float(min(IDX_TOPK, seq // 4))),
        grid=(nb, nq),
        in_specs=[pl.BlockSpec((1024, tq), lambda b, i: (0, b * nq + i)),
                  pl.BlockSpec((IDX_H, tq), lambda b, i: (0, b * nq + i)),
                  pl.BlockSpec((seq, 128), lambda b, i: (b, 0)),
                  pl.BlockSpec((seq // tk, 64, tk), lambda b, i: (b, 0, 0))],
        out_specs=pl.BlockSpec((tq, BRANCH_W), lambda b, i: (b * nq + i, 0)),
        out_shape=jax.ShapeDtypeStruct((nb * seq, BRANCH_W), BF16),
        scratch_shapes=[pltpu.VMEM((seq // tk, tk, tq), I32), pltpu.VMEM((1, tq), I32),
                        pltpu.VMEM((1, m), F32), pltpu.VMEM((1, m), F32), pltpu.VMEM((DSA_D, m), F32)],
        compiler_params=_cparams(("parallel", "arbitrary")),
        name="dsa_prompt",
    )(qi_t, iw_t, k, v_t)


def _page_specs(layer, block, rows_per_step, pages_per_step):
    def spec(g, i):
        return pl.BlockSpec((None, None) + block,
                            lambda b, s, pt: (layer, pt[b * rows_per_step + g, s * pages_per_step + i], 0, 0))
    return [spec(g, i) for g in range(rows_per_step) for i in range(pages_per_step)]


def _group_sum(x, group=8):
    n = x.shape[0] // group
    return jnp.concatenate([jnp.sum(x[i * group:(i + 1) * group], axis=0, keepdims=True) for i in range(n)], axis=0)


def _mla_sample_kernel(pt_ref, q_ref, knew_ref, wuv_ref, *rest, n_rows, n_pages_step, n_steps, dec_seq):
    pages = rest[:n_rows * n_pages_step]
    o_ref, kbuf, m_scr, l_scr, acc_scr = rest[n_rows * n_pages_step:]
    step = pl.program_id(1)
    rows = dec_seq * MLA_H

    @pl.when(step == 0)
    def _():
        _init_state(m_scr, l_scr, acc_scr)
        kbuf[...] = jnp.zeros(kbuf.shape, BF16)

    for g in range(n_rows):
        for i in range(n_pages_step):
            kbuf[g, 0:MLA_W, i * PAGE:(i + 1) * PAGE] = pages[g * n_pages_step + i][...].astype(BF16)
        _softmax_step_t(_dot(q_ref[g], kbuf[g]), kbuf[g, 0:128, :], m_scr.at[g], l_scr.at[g], acc_scr.at[g])

    @pl.when(step == n_steps - 1)
    def _():
        for g in range(n_rows):
            qf = q_ref[g].astype(F32)
            knew = knew_ref[g].astype(BF16).astype(F32)
            qrow = _row_iota((rows, 1)) // MLA_H
            cols, vrows = [], []
            for j in range(dec_seq):
                c = jnp.sum(qf[:, 0:MLA_W] * knew[j:j + 1, :], axis=-1, keepdims=True)
                cols.append(jnp.where(qrow >= j, c, NEG))
                vrows.append(knew[j:j + 1, 0:128])
            _softmax_cols(cols, vrows, m_scr.at[g], l_scr.at[g], acc_scr.at[g])
            o = (acc_scr[g] / l_scr[g]).astype(BF16)
            oall = _dot(o, wuv_ref[...])
            keep = (_lane_iota((rows, BRANCH_W)) // MLA_V) == (_row_iota((rows, BRANCH_W)) % MLA_H)
            o_ref[g] = _group_sum(jnp.where(keep, oall, 0.0), MLA_H).astype(BF16)


def _mla_sample_call(page_table, q, knew, wuv_cat, cache_t, layer, rps, pps):
    nb, n_pages = page_table.shape
    n_steps = n_pages // pps
    dec_seq = knew.shape[1]
    rows = dec_seq * MLA_H
    grid_spec = pltpu.PrefetchScalarGridSpec(
        num_scalar_prefetch=1,
        grid=(nb // rps, n_steps),
        in_specs=[pl.BlockSpec((rps, rows, 256), lambda b, s, pt: (b, 0, 0)),
                  pl.BlockSpec((rps, dec_seq, MLA_W), lambda b, s, pt: (b, 0, 0)),
                  pl.BlockSpec(wuv_cat.shape, lambda b, s, pt: (0, 0))]
        + _page_specs(layer, (MLA_W, PAGE), rps, pps),
        out_specs=pl.BlockSpec((rps, dec_seq, BRANCH_W), lambda b, s, pt: (b, 0, 0)),
        scratch_shapes=[pltpu.VMEM((rps, 256, pps * PAGE), BF16), pltpu.VMEM((rps, rows, 1), F32),
                        pltpu.VMEM((rps, rows, 1), F32), pltpu.VMEM((rps, rows, 128), F32)],
    )
    return pl.pallas_call(
        functools.partial(_mla_sample_kernel, n_rows=rps, n_pages_step=pps, n_steps=n_steps, dec_seq=dec_seq),
        grid_spec=grid_spec,
        out_shape=jax.ShapeDtypeStruct((nb, dec_seq, BRANCH_W), BF16),
        compiler_params=_cparams(("parallel", "arbitrary")),
        name="mla_sample",
    )(page_table, q, knew, wuv_cat, *([cache_t] * (rps * pps)))


def _diff_sample_kernel(pt_ref, q_ref, knew_ref, lam_ref, g_ref, *rest, n_rows, n_pages_step, n_steps, dec_seq,
                        lam_init):
    pages = rest[:n_rows * n_pages_step]
    o_ref, kbuf, m_scr, l_scr, acc_scr = rest[n_rows * n_pages_step:]
    step = pl.program_id(1)
    rows = dec_seq * 2 * DIFF_H

    @pl.when(step == 0)
    def _():
        _init_state(m_scr, l_scr, acc_scr)

    for g in range(n_rows):
        for i in range(n_pages_step):
            kbuf[g, i * PAGE:(i + 1) * PAGE, :] = pages[g * n_pages_step + i][...].astype(BF16)
        _softmax_step(_dot_t(q_ref[g], kbuf[g, :, 0:128]), kbuf[g, :, 128:256], m_scr.at[g], l_scr.at[g],
                      acc_scr.at[g])

    @pl.when(step == n_steps - 1)
    def _():
        lam = _diff_lambda(lam_ref, lam_init)
        for g in range(n_rows):
            qf = q_ref[g].astype(F32)
            knew = knew_ref[g].astype(BF16).astype(F32)
            qrow = _row_iota((rows, 1)) // (2 * DIFF_H)
            cols, vrows = [], []
            for j in range(dec_seq):
                c = jnp.sum(qf * knew[j:j + 1, 0:128], axis=-1, keepdims=True)
                cols.append(jnp.where(qrow >= j, c, NEG))
                vrows.append(knew[j:j + 1, 128:256])
            _softmax_cols(cols, vrows, m_scr.at[g], l_scr.at[g], acc_scr.at[g])
            o = acc_scr[g] / l_scr[g]
            d = o - lam * pltpu.roll(o, rows - 1, 0)
            dn = d * lax.rsqrt(jnp.mean(d * d, axis=-1, keepdims=True) + EPS) * g_ref[...] * (1.0 - lam_init)
            o_ref[g] = dn.astype(BF16)


def _diff_sample_call(page_table, q, knew, lam_p, subln, cache, layer, rps, pps, lam_init):
    nb, n_pages = page_table.shape
    n_steps = n_pages // pps
    dec_seq = knew.shape[1]
    rows = dec_seq * 2 * DIFF_H
    grid_spec = pltpu.PrefetchScalarGridSpec(
        num_scalar_prefetch=1,
        grid=(nb // rps, n_steps),
        in_specs=[pl.BlockSpec((rps, rows, 128), lambda b, s, pt: (b, 0, 0)),
                  pl.BlockSpec((rps, dec_seq, DIFF_W), lambda b, s, pt: (b, 0, 0)),
                  pl.BlockSpec(lam_p.shape, lambda b, s, pt: (0, 0)),
                  pl.BlockSpec(subln.shape, lambda b, s, pt: (0, 0))]
        + _page_specs(layer, (PAGE, DIFF_W), rps, pps),
        out_specs=pl.BlockSpec((rps, rows, 128), lambda b, s, pt: (b, 0, 0)),
        scratch_shapes=[pltpu.VMEM((rps, pps * PAGE, 256), BF16), pltpu.VMEM((rps, rows, 1), F32),
                        pltpu.VMEM((rps, rows, 1), F32), pltpu.VMEM((rps, rows, 128), F32)],
    )
    return pl.pallas_call(
        functools.partial(_diff_sample_kernel, n_rows=rps, n_pages_step=pps, n_steps=n_steps, dec_seq=dec_seq,
                          lam_init=lam_init),
        grid_spec=grid_spec,
        out_shape=jax.ShapeDtypeStruct((nb, rows, 128), BF16),
        compiler_params=_cparams(("parallel", "arbitrary")),
        name="diff_sample",
    )(page_table, q, knew, lam_p, subln, *([cache] * (rps * pps)))


def _dsa_sample_kernel(pt_ref, lhs_ref, iw_ref, knew_ref, *rest, n_rows, n_pages_step, n_steps, dec_seq, topk):
    pages = rest[:n_rows * n_pages_step]
    o_ref, kbuf, s_scr, v_scr, key_scr, knew_key_scr, plim_scr = rest[n_rows * n_pages_step:]
    step = pl.program_id(1)
    rows = dec_seq * DSA_H
    pk = n_pages_step * PAGE
    hk = pk // 2
    past = n_steps * pk

    @pl.when(step == 0)
    def _():
        kbuf[...] = jnp.zeros(kbuf.shape, BF16)

    for g in range(n_rows):
        for i in range(n_pages_step):
            kbuf[g, 0:DSA_W, i * PAGE:(i + 1) * PAGE] = pages[g * n_pages_step + i][...].astype(BF16)
        lg = _dot(lhs_ref[g], kbuf[g])
        s_scr[g, step] = lg[0:rows]
        v_scr[g, step] = kbuf[g, DSA_D:2 * DSA_D, :]
        key = _sort_key(_group_sum(jnp.maximum(lg[rows:2 * rows], 0.0) * iw_ref[g], IDX_H))
        key_scr[g, step] = jnp.concatenate([key[:, 0:hk], key[:, hk:pk]], axis=0)

    @pl.when(step == n_steps - 1)
    def _():
        for g in range(n_rows):
            _dsa_sample_finish(g, lhs_ref, iw_ref, knew_ref, o_ref, s_scr, v_scr, key_scr, knew_key_scr, plim_scr,
                               n_steps=n_steps, dec_seq=dec_seq, rows=rows, pk=pk, hk=hk, past=past, topk=topk)


def _dsa_sample_finish(g, lhs_ref, iw_ref, knew_ref, o_ref, s_scr, v_scr, key_scr, knew_key_scr, plim_scr,
                       *, n_steps, dec_seq, rows, pk, hk, past, topk):
    lhs = lhs_ref[g].astype(F32)
    knew = knew_ref[g].astype(BF16).astype(F32)
    iw = iw_ref[g]
    lane_q = _lane_iota((dec_seq, LANES))
    row_q = _row_iota((dec_seq, LANES))
    sc_new = jnp.full((dec_seq, LANES), -jnp.inf, F32)
    s_new = []
    for j in range(dec_seq):
        s_new.append(jnp.sum(lhs[0:rows, 0:DSA_D] * knew[j:j + 1, 0:DSA_D], axis=-1, keepdims=True))
        il = jnp.sum(lhs[rows:2 * rows, 2 * DSA_D:3 * DSA_D] * knew[j:j + 1, 2 * DSA_D:3 * DSA_D],
                     axis=-1, keepdims=True)
        sc = _group_sum(jnp.maximum(il, 0.0) * iw, IDX_H)
        sc_new = jnp.where((lane_q == j) & (row_q >= j), sc, sc_new)
    knew_key_scr[...] = _sort_key(sc_new)
    pos0 = _lane_iota((2 * dec_seq, hk)) + (_row_iota((2 * dec_seq, hk)) // dec_seq) * hk

    def fold(x):
        return x[0:dec_seq] + x[dec_seq:2 * dec_seq]

    def both(x):
        return jnp.concatenate([x, x], axis=0)

    def count_fn(pred):
        def body(i, c):
            return c + jnp.where(pred(key_scr[g, i], pos0 + i * pk), 1, 0)
        c = lax.fori_loop(0, n_steps, body, jnp.zeros((2 * dec_seq, hk), I32))
        cn = jnp.where(pred(jnp.concatenate([knew_key_scr[...]] * 2, axis=0)[:, 0:LANES],
                            jnp.concatenate([lane_q + past, lane_q + past], axis=0)), 1.0, 0.0)
        total = jnp.sum(c.astype(F32), axis=-1, keepdims=True)
        return both(fold(total) + jnp.sum(cn[0:dec_seq], axis=-1, keepdims=True))

    t, need, tie = _select_threshold(count_fn, (2 * dec_seq, 1), topk)
    plim_scr[...] = jnp.full((2 * dec_seq, 1), 2 ** IDX_BITS, I32)

    @pl.when(jnp.max(jnp.where(tie, 1.0, 0.0)) > 0.0)
    def _():
        plim_scr[...] = _tie_limit(count_fn, t, need, (2 * dec_seq, 1))

    plim = plim_scr[...]

    def expand(x):
        return jnp.concatenate([jnp.broadcast_to(x[i:i + 1], (DSA_H, x.shape[1])) for i in range(dec_seq)], axis=0)

    bias_new = expand(jnp.where(_selected(knew_key_scr[...], lane_q + past, t[0:dec_seq], plim[0:dec_seq]),
                                0.0, NEG))
    cols_new = [s_new[j] + bias_new[:, j:j + 1] for j in range(dec_seq)]
    m = cols_new[0]
    for c in cols_new[1:]:
        m = jnp.maximum(m, c)
    for i in range(n_steps):
        sel = _selected(key_scr[g, i], pos0 + i * pk, t, plim)
        bias = jnp.where(sel, 0.0, NEG)
        bias = jnp.concatenate([bias[0:dec_seq], bias[dec_seq:2 * dec_seq]], axis=1)
        sb = s_scr[g, i] + expand(bias)
        s_scr[g, i] = sb
        m = jnp.maximum(m, jnp.max(sb, axis=-1, keepdims=True))
    l = jnp.zeros((rows, 1), F32)
    acc = jnp.zeros((rows, DSA_D), F32)
    for i in range(n_steps):
        p = jnp.exp(s_scr[g, i] - m)
        l = l + jnp.sum(p, axis=-1, keepdims=True)
        acc = acc + _dot_t(p.astype(BF16), v_scr[g, i])
    for j in range(dec_seq):
        p = jnp.exp(cols_new[j] - m)
        l = l + p
        acc = acc + p * knew[j:j + 1, DSA_D:2 * DSA_D]
    o_ref[g] = (acc / l).astype(BF16)


def _dsa_sample_call(page_table, lhs, iw, knew, cache_t, layer, rps, pps):
    nb, n_pages = page_table.shape
    n_steps = n_pages // pps
    dec_seq = knew.shape[1]
    rows = dec_seq * DSA_H
    pk = pps * PAGE
    grid_spec = pltpu.PrefetchScalarGridSpec(
        num_scalar_prefetch=1,
        grid=(nb // rps, n_steps),
        in_specs=[pl.BlockSpec((rps, 2 * rows, 256), lambda b, s, pt: (b, 0, 0)),
                  pl.BlockSpec((rps, rows, 1), lambda b, s, pt: (b, 0, 0)),
                  pl.BlockSpec((rps, dec_seq, DSA_W), lambda b, s, pt: (b, 0, 0))]
        + _page_specs(layer, (DSA_W, PAGE), rps, pps),
        out_specs=pl.BlockSpec((rps, rows, DSA_D), lambda b, s, pt: (b, 0, 0)),
        scratch_shapes=[pltpu.VMEM((rps, 256, pk), BF16),
                        pltpu.VMEM((rps, n_steps, rows, pk), F32),
                        pltpu.VMEM((rps, n_steps, DSA_D, pk), BF16),
                        pltpu.VMEM((rps, n_steps, 2 * dec_seq, pk // 2), I32),
                        pltpu.VMEM((dec_seq, LANES), I32),
                        pltpu.VMEM((2 * dec_seq, 1), I32)],
    )
    return pl.pallas_call(
        functools.partial(_dsa_sample_kernel, n_rows=rps, n_pages_step=pps, n_steps=n_steps, dec_seq=dec_seq,
                          topk=float(min(IDX_TOPK, (n_pages * PAGE + dec_seq) // 4))),
        grid_spec=grid_spec,
        out_shape=jax.ShapeDtypeStruct((nb, rows, DSA_D), BF16),
        compiler_params=_cparams(("parallel", "arbitrary")),
        name="dsa_sample",
    )(page_table, lhs, iw, knew, *([cache_t] * (rps * pps)))


def _layer_norm(z, g, b):
    mu = jnp.mean(z, axis=-1, keepdims=True)
    zc = z - mu
    var = jnp.mean(zc * zc, axis=-1, keepdims=True)
    return zc * lax.rsqrt(var + EPS) * g + b


def _merge_kernel(x_ref, om_ref, od_ref, os_ref, wg_ref, bg_ref, wb_ref, wo_ref, g1_ref, b1_ref,
                  wrh_ref, wrl_ref, br_ref, y_ref, comb_ref, *, alpha):
    tm = x_ref.shape[0]
    x = x_ref[...]
    xb = x.astype(BF16)
    mix = None
    for n, o_ref in enumerate((om_ref, od_ref, os_ref)):
        z = _dot(xb, wg_ref[:, n * D_MODEL:(n + 1) * D_MODEL]) + bg_ref[:, n * D_MODEL:(n + 1) * D_MODEL]
        gate = 1.0 / (1.0 + jnp.exp(-z))
        term = gate * _dot(o_ref[...], wb_ref[n])
        mix = term if mix is None else mix + term
    mixed = _dot(mix.astype(BF16), wo_ref[...])
    y = _layer_norm(alpha * x + mixed, g1_ref[...], b1_ref[...])
    y_ref[...] = y

    yh = y.astype(BF16)
    yl = (y - yh.astype(F32)).astype(BF16)
    logit = _dot(yh, wrh_ref[...]) + (_dot(yh, wrl_ref[...]) + _dot(yl, wrh_ref[...])) + br_ref[...]
    lane_i = _lane_iota((tm, LANES))
    lane = lane_i.astype(F32)
    big = float(LANES)
    gl = jnp.where(lane_i < N_GROUPS, logit, -jnp.inf)
    gmax = jnp.max(gl, axis=-1, keepdims=True)
    pg = jnp.exp(gl - gmax)
    pg = pg / jnp.sum(pg, axis=-1, keepdims=True)
    pg_star = jnp.max(pg, axis=-1, keepdims=True)
    g_star = jnp.min(jnp.where(pg == pg_star, lane, big), axis=-1, keepdims=True)
    group_of = lax.shift_right_arithmetic(lane_i - N_GROUPS, 3).astype(F32)
    in_group = (lane_i >= N_GROUPS) & (group_of == g_star)
    el = jnp.where(in_group, logit, -jnp.inf)
    pe = jnp.exp(el - jnp.max(el, axis=-1, keepdims=True))
    pe = pe / jnp.sum(pe, axis=-1, keepdims=True)
    pe = jnp.where(in_group, pe, -1.0)
    w1 = jnp.max(pe, axis=-1, keepdims=True)
    i1 = jnp.min(jnp.where(pe == w1, lane, big), axis=-1, keepdims=True)
    pe2 = jnp.where(lane == i1, -1.0, pe)
    w2 = jnp.max(pe2, axis=-1, keepdims=True)
    i2 = jnp.min(jnp.where(pe2 == w2, lane, big), axis=-1, keepdims=True)
    wsum = w1 + w2
    comb_ref[...] = jnp.where(lane == i1, pg_star * w1 / wsum, 0.0) + jnp.where(lane == i2, pg_star * w2 / wsum, 0.0)


def _merge_call(x, om, od, os_, wg, bg, wb, wo, g1, b1, wrh, wrl, br, tm, alpha):
    t = x.shape[0]
    row = lambda w: pl.BlockSpec((tm, w), lambda i: (i, 0))
    full = lambda a: pl.BlockSpec(a.shape, lambda i: (0,) * a.ndim)
    return pl.pallas_call(
        functools.partial(_merge_kernel, alpha=alpha),
        grid=(t // tm,),
        in_specs=[row(D_MODEL), row(BRANCH_W), row(BRANCH_W), row(BRANCH_W), full(wg), full(bg), full(wb),
                  full(wo), full(g1), full(b1), full(wrh), full(wrl), full(br)],
        out_specs=[row(D_MODEL), row(LANES)],
        out_shape=[jax.ShapeDtypeStruct((t, D_MODEL), F32), jax.ShapeDtypeStruct((t, LANES), F32)],
        compiler_params=_cparams(("parallel",)),
        name="merge",
    )(x, om, od, os_, wg, bg, wb, wo, g1, b1, wrh, wrl, br)


def _moe_kernel(y_ref, comb_ref, win_ref, wout_ref, g2_ref, b2_ref, o_ref, xb_scr, acc_scr, *, alpha):
    e = pl.program_id(1)
    tm = y_ref.shape[0]

    @pl.when(e == 0)
    def _():
        xb_scr[...] = y_ref[...].astype(BF16)
        acc_scr[...] = jnp.zeros(acc_scr.shape, F32)

    h = _dot(xb_scr[...], win_ref[...])
    lane = _lane_iota((tm, LANES))
    c = jnp.sum(jnp.where(lane == e + N_GROUPS, comb_ref[...], 0.0), axis=-1, keepdims=True)
    h1 = h[:, 0:D_FF]
    act = h1 / (1.0 + jnp.exp(-h1)) * h[:, D_FF:2 * D_FF] * c
    acc_scr[...] += _dot(act.astype(BF16), wout_ref[...])

    @pl.when(e == N_EXPERTS - 1)
    def _():
        o_ref[...] = _layer_norm(alpha * y_ref[...] + acc_scr[...], g2_ref[...], b2_ref[...])


def _moe_call(y, comb, win, wout, g2, b2, tm, alpha):
    t = y.shape[0]
    return pl.pallas_call(
        functools.partial(_moe_kernel, alpha=alpha),
        grid=(t // tm, N_EXPERTS),
        in_specs=[pl.BlockSpec((tm, D_MODEL), lambda i, e: (i, 0)),
                  pl.BlockSpec((tm, LANES), lambda i, e: (i, 0)),
                  pl.BlockSpec((None, D_MODEL, 2 * D_FF), lambda i, e: (e, 0, 0)),
                  pl.BlockSpec((None, D_FF, D_MODEL), lambda i, e: (e, 0, 0)),
                  pl.BlockSpec(g2.shape, lambda i, e: (0, 0)),
                  pl.BlockSpec(b2.shape, lambda i, e: (0, 0))],
        out_specs=pl.BlockSpec((tm, D_MODEL), lambda i, e: (i, 0)),
        out_shape=jax.ShapeDtypeStruct((t, D_MODEL), F32),
        scratch_shapes=[pltpu.VMEM((tm, D_MODEL), BF16), pltpu.VMEM((tm, D_MODEL), F32)],
        compiler_params=_cparams(("parallel", "arbitrary")),
        name="moe",
    )(y, comb, win, wout, g2, b2)


def _in_proj_rows(w):
    widths = (MLA_Q_LORA, MLA_KV_LORA, MLA_ROPE, DIFF_H * 2 * DIFF_D, 2 * DIFF_D, 2 * DIFF_D,
              DSA_H * DSA_D, DSA_D, DSA_D, IDX_H * IDX_D, IDX_D, IDX_H)
    off = np.concatenate([[0], np.cumsum(widths)]).astype(int)
    c_q, c_kv, kr, dq, dk, dv, sq, sk, sv, iq, ik, iw = [w[:, off[i]:off[i + 1]] for i in range(12)]
    qi = []
    for h in range(DSA_H):
        qi += [sq[:, h * 64:(h + 1) * 64] * DSA_SCALE, iq[:, h * 64:(h + 1) * 64]]
    pad = jnp.zeros((w.shape[0], R_CKV - R_G2 - 3 * 64 - MLA_ROPE - IDX_H), F32)
    cols = jnp.concatenate([c_q, dq * DIFF_SCALE] + qi + [dk, dv, sk, ik, sv, kr, iw, pad, c_kv], axis=1)
    assert cols.shape[1] == IN_ROWS
    return cols.T.astype(BF16)


def _rope_tables_t(pos):
    posf = pos.astype(F32)[None, :]
    inv64 = 1.0 / (ROPE_THETA ** (jnp.arange(32, dtype=F32) / 32))
    a64 = inv64[:, None] * posf
    inv32 = 1.0 / (ROPE_THETA ** (jnp.arange(16, dtype=F32) / 16))
    a32 = inv32[:, None] * posf
    return jnp.cos(a64), jnp.sin(a64), jnp.cos(a32), jnp.sin(a32)


def _pick_tile(t, candidates):
    for c in candidates:
        if t % c == 0:
            return c
    raise ValueError(f"no tile for {t}")


def kernel(x_prompt, x_sample, cache_mla, cache_diff, cache_dsa, page_table, w_in, mla_gq, mla_gkv, mla_wuq,
           mla_wukv, diff_lam, diff_subln, w_gate, b_gate, w_branch, w_o, ln1_g, ln1_b, w_rg, b_rg, w_re, b_re,
           w_e_in, w_e_out, ln2_g, ln2_b):
    nb, seq, d = x_prompt.shape
    db, dec_seq, _ = x_sample.shape
    depth = w_in.shape[0]
    n_pages = page_table.shape[1]
    past_len = n_pages * PAGE
    tp, ts = nb * seq, db * dec_seq
    t = tp + ts
    alpha = (2 * depth) ** 0.25
    assert d == D_MODEL and seq % TOK_TILE == 0 and t % TOK_TILE == 0

    tq = 128
    tm = TOK_TILE
    tm_moe = _pick_tile(t, (1536, 1024, 768, 512, 256, 128))
    pps = _pick_tile(n_pages, (16, 8, 4, 2, 1))
    rps = _pick_tile(db, (4, 2, 1))

    pos = jnp.concatenate([jnp.tile(jnp.arange(seq, dtype=I32), nb),
                           jnp.tile(past_len + jnp.arange(dec_seq, dtype=I32), db)])
    c64, s64, c32, s32 = _rope_tables_t(pos)
    x = jnp.concatenate([x_prompt.reshape(tp, d), x_sample.reshape(ts, d)], axis=0)
    outs = [[] for _ in range(6)]
    cache_mla_t = jnp.swapaxes(cache_mla, 2, 3)
    cache_dsa_t = jnp.swapaxes(cache_dsa, 2, 3)

    for l in range(depth):
        lam_init = 0.8 - 0.6 * math.exp(-0.3 * l)
        w_in_t = _in_proj_rows(w_in[l])
        wuq = mla_wuq[l].reshape(MLA_Q_LORA, MLA_H, MLA_NOPE + MLA_ROPE)
        wukv = mla_wukv[l].reshape(MLA_KV_LORA, MLA_H, MLA_NOPE + MLA_V)
        w_abs = jnp.einsum('rhn,chn->rhc', wuq[..., :MLA_NOPE], wukv[..., :MLA_NOPE],
                           precision=lax.Precision.HIGHEST)
        wq = jnp.concatenate([w_abs, wuq[..., MLA_NOPE:], jnp.zeros((MLA_Q_LORA, MLA_H, 96), F32)], axis=-1)
        wq_t = (wq * MLA_SCALE).reshape(MLA_Q_LORA, MLA_H * 256).T.astype(BF16)
        wuv_t = jnp.transpose(wukv[..., MLA_NOPE:], (1, 2, 0)).astype(BF16)
        wuv_cat = wukv[..., MLA_NOPE:].reshape(MLA_KV_LORA, MLA_H * MLA_V).astype(BF16)
        wr = jnp.concatenate([w_rg[l], w_re[l], jnp.zeros((d, LANES - N_GROUPS - N_EXPERTS), F32)], axis=1)
        wrh = wr.astype(BF16)
        wrl = (wr - wrh.astype(F32)).astype(BF16)
        br = jnp.concatenate([b_rg[l], b_re[l], jnp.zeros((LANES - N_GROUPS - N_EXPERTS,), F32)])[None, :]

        (nmla_t, ndiff, ndsa_t, qmla_t, kmla, vmla_t, qdiff_t, kdiff, vdiff_t, qidsa_t, kdsa, vdsa_t,
         iw_t) = _proj_call(x, w_in_t, wq_t, mla_gq[l][:, None], mla_gkv[l][:, None], c64, s64, c32, s32)

        om_p = _mla_prompt_call(qmla_t, kmla, vmla_t, wuv_t, nb, seq, tq)
        od_p = _diff_prompt_call(qdiff_t, kdiff, vdiff_t, diff_lam[l], diff_subln[l][:, None], nb, seq, tq, lam_init)
        os_p = _dsa_prompt_call(qidsa_t, iw_t, kdsa, vdsa_t, nb, seq, tq)

        nmla_s = nmla_t[:, tp:].T.reshape(db, dec_seq, MLA_W)
        ndsa_s = ndsa_t[:, tp:].T.reshape(db, dec_seq, DSA_W)
        q_s = qmla_t[:, tp:].T.reshape(db, dec_seq * MLA_H, 256)
        om_s = _mla_sample_call(page_table, q_s, nmla_s, wuv_cat, cache_mla_t, l, rps, pps)
        qd = qdiff_t[:, tp:].T.reshape(db, dec_seq, DIFF_H, 2, DIFF_D)
        zd = jnp.zeros_like(qd[..., 0, :])
        qd = jnp.stack([jnp.concatenate([qd[..., 0, :], zd], axis=-1),
                        jnp.concatenate([zd, qd[..., 1, :]], axis=-1)], axis=3)
        od_s = _diff_sample_call(page_table, qd.reshape(db, dec_seq * 2 * DIFF_H, 128),
                                 ndiff[tp:].reshape(db, dec_seq, DIFF_W), diff_lam[l], diff_subln[l][None, :],
                                 cache_diff, l, rps, pps, lam_init)
        od_s = od_s.reshape(db, dec_seq, DIFF_H, 2, 128)[:, :, :, 0, :]
        qi = qidsa_t[:, tp:].T.reshape(db, dec_seq * DSA_H, 128)
        z64 = jnp.zeros_like(qi[..., :64])
        lhs = jnp.concatenate([jnp.concatenate([qi[..., :64], z64, z64, z64], axis=-1),
                               jnp.concatenate([z64, z64, qi[..., 64:], z64], axis=-1)], axis=1)
        os_s = _dsa_sample_call(page_table, lhs, iw_t[:, tp:].T.reshape(db, dec_seq * IDX_H, 1), ndsa_s,
                                cache_dsa_t, l, rps, pps)

        om = jnp.concatenate([om_p, om_s.reshape(ts, BRANCH_W)], axis=0)
        od = jnp.concatenate([od_p, od_s.reshape(ts, BRANCH_W)], axis=0)
        os_ = jnp.concatenate([os_p, os_s.reshape(ts, BRANCH_W)], axis=0)

        y, comb = _merge_call(x, om, od, os_, w_gate[l].astype(BF16), b_gate[l][None, :], w_branch[l].astype(BF16),
                              w_o[l].astype(BF16), ln1_g[l][None, :], ln1_b[l][None, :], wrh, wrl, br, tm, alpha)
        x = _moe_call(y, comb, w_e_in[l].astype(BF16), w_e_out[l].astype(BF16), ln2_g[l][None, :],
                      ln2_b[l][None, :], tm_moe, alpha)

        outs[0].append(nmla_t[:, :tp].T.reshape(nb, seq, MLA_W))
        outs[1].append(nmla_s)
        outs[2].append(ndiff[:tp].reshape(nb, seq, DIFF_W))
        outs[3].append(ndiff[tp:].reshape(db, dec_seq, DIFF_W))
        outs[4].append(ndsa_t[:, :tp].T.reshape(nb, seq, DSA_W))
        outs[5].append(ndsa_s)

    return (x[:tp].reshape(nb, seq, d), x[tp:].reshape(db, dec_seq, d),
            jnp.stack(outs[0]), jnp.stack(outs[1]), jnp.stack(outs[2]), jnp.stack(outs[3]),
            jnp.stack(outs[4]), jnp.stack(outs[5]))
```
